```python
import jax, jax.numpy as jnp
from jax import lax
import numpy as np

D_MODEL = 1024
BATCH = 32
SEQ = 2048
DEPTH = 1

PLE_DIM = 256
HEAD_DIM = 64
MOBA_HEADS = 8
MOBA_BLOCK = 256
MOBA_TOPK = 3
MOBA_Q_CHUNK = 16
DIFF_HEADS = 4
DIFF_V_DIM = 2 * HEAD_DIM
ROPE_THETA = 500000.0
ROPE_DIM = HEAD_DIM // 4
DENSE_Q_BLOCK = 128
EPS = 1e-6
MOBA_W = MOBA_HEADS * HEAD_DIM
DIFF_QK_W = DIFF_HEADS * 2 * HEAD_DIM
DIFF_W = DIFF_HEADS * DIFF_V_DIM
MIX_W = MOBA_W + DIFF_W
SPLIT_SIZES = (MOBA_W, MOBA_W, MOBA_W, DIFF_QK_W, DIFF_QK_W, DIFF_W, MIX_W)
PROJ_W = sum(SPLIT_SIZES)

kernel_name = "hybrid_moba_diffattn_gated_ple"


def rmsnorm(x, g):
    xf = x.astype(jnp.float32)
    xf = xf * lax.rsqrt(jnp.mean(xf * xf, axis=-1, keepdims=True) + EPS)
    return (xf * g.astype(jnp.float32)).astype(x.dtype)


def rope_tables(seq):
    inv = ROPE_THETA ** (-jnp.arange(0, ROPE_DIM, 2, dtype=jnp.float32) / ROPE_DIM)
    ang = jnp.arange(seq, dtype=jnp.float32)[:, None] * inv[None, :]
    return jnp.cos(ang), jnp.sin(ang)


def partial_rope(x, cos, sin):
    half = ROPE_DIM // 2
    x1, x2, rest = x[..., :half], x[..., half:ROPE_DIM], x[..., ROPE_DIM:]
    c, s = cos.astype(x.dtype), sin.astype(x.dtype)
    return jnp.concatenate([x1 * c - x2 * s, x2 * c + x1 * s, rest], axis=-1)


def moba_attention(q, k, v, cos, sin):
    B, H, S, dh = q.shape
    q = partial_rope(q, cos, sin)
    k = partial_rope(k, cos, sin)
    nb = -(-S // MOBA_BLOCK)
    pad = nb * MOBA_BLOCK - S
    kp = jnp.pad(k, ((0, 0), (0, 0), (0, pad), (0, 0)))
    vp = jnp.pad(v, ((0, 0), (0, 0), (0, pad), (0, 0)))
    kb = kp.reshape(B, H, nb, MOBA_BLOCK, dh)
    vb = vp.reshape(B, H, nb, MOBA_BLOCK, dh)
    kmean = jnp.mean(kb.astype(jnp.float32), axis=3)
    n_sel = min(MOBA_TOPK, nb)
    scale = dh ** -0.5
    bi = jnp.arange(B)[:, None, None, None]
    hi = jnp.arange(H)[None, :, None, None]
    blk_ids = jnp.arange(nb)
    key_off = jnp.arange(MOBA_BLOCK)

    def chunk(c):
        start = c * MOBA_Q_CHUNK
        qc = lax.dynamic_slice_in_dim(q, start, MOBA_Q_CHUNK, axis=2)
        qpos = start + jnp.arange(MOBA_Q_CHUNK)
        own = start // MOBA_BLOCK
        gate = jnp.einsum('bhqd,bhnd->bhqn', qc.astype(jnp.float32), kmean)
        gate = jnp.where(blk_ids < own, gate, -jnp.inf)
        _, sel = lax.top_k(gate, n_sel)
        sel_ok = sel < own
        ks = kb[bi, hi, sel]
        vs = vb[bi, hi, sel]
        s_sel = jnp.einsum('bhqd,bhqnkd->bhqnk', qc, ks).astype(jnp.float32) * scale
        s_sel = jnp.where(sel_ok[..., None], s_sel, -jnp.inf)
        s_sel = s_sel.reshape(B, H, MOBA_Q_CHUNK, n_sel * MOBA_BLOCK)
        ko = lax.dynamic_slice_in_dim(kp, own * MOBA_BLOCK, MOBA_BLOCK, axis=2)
        vo = lax.dynamic_slice_in_dim(vp, own * MOBA_BLOCK, MOBA_BLOCK, axis=2)
        kpos = own * MOBA_BLOCK + key_off
        s_own = jnp.einsum('bhqd,bhkd->bhqk', qc, ko).astype(jnp.float32) * scale
        s_own = jnp.where(kpos[None, :] <= qpos[:, None], s_own, -jnp.inf)
        probs = jax.nn.softmax(jnp.concatenate([s_sel, s_own], axis=-1), axis=-1).astype(v.dtype)
        p_sel = probs[..., :n_sel * MOBA_BLOCK].reshape(B, H, MOBA_Q_CHUNK, n_sel, MOBA_BLOCK)
        p_own = probs[..., n_sel * MOBA_BLOCK:]
        return (jnp.einsum('bhqnk,bhqnkd->bhqd', p_sel, vs)
                + jnp.einsum('bhqk,bhkd->bhqd', p_own, vo))

    outs = lax.map(chunk, jnp.arange(S // MOBA_Q_CHUNK))
    return outs.transpose(1, 2, 0, 3, 4).reshape(B, H, S, dh)


def diff_attention(q, k, v, lam, g_subln, lambda_init, cos, sin):
    B, H, _, S, dh = q.shape
    q = partial_rope(q, cos, sin)
    k = partial_rope(k, cos, sin)
    scale = dh ** -0.5
    kpos = jnp.arange(S)

    def block(c):
        start = c * DENSE_Q_BLOCK
        qc = lax.dynamic_slice_in_dim(q, start, DENSE_Q_BLOCK, axis=3)
        s = jnp.einsum('bhiqd,bhikd->bhiqk', qc, k).astype(jnp.float32) * scale
        qpos = start + jnp.arange(DENSE_Q_BLOCK)
        s = jnp.where(kpos[None, :] <= qpos[:, None], s, -jnp.inf)
        pr = jax.nn.softmax(s, axis=-1)
        a = pr[:, :, 0] - lam * pr[:, :, 1]
        return jnp.einsum('bhqk,bhkd->bhqd', a.astype(v.dtype), v)

    o = lax.map(block, jnp.arange(S // DENSE_Q_BLOCK))
    o = o.transpose(1, 2, 0, 3, 4).reshape(B, H, S, 2 * dh)
    return rmsnorm(o, g_subln) * (1.0 - lambda_init)


def setup_inputs(seed: int = 0) -> dict:
    key = jax.random.key(seed)
    ks = jax.random.split(key, 14)
    f32 = jnp.float32
    nrm = lambda k, shape, s: jax.random.normal(k, shape, f32) * s
    return {
        "x": nrm(ks[0], (BATCH, SEQ, D_MODEL), 1.0),
        "p": nrm(ks[1], (DEPTH, BATCH, SEQ, PLE_DIM), 1.0),
        "w_in": nrm(ks[2], (DEPTH, D_MODEL, PROJ_W), D_MODEL ** -0.5),
        "w_out": nrm(ks[3], (DEPTH, MIX_W, D_MODEL), MIX_W ** -0.5),
        "g_mix": 1.0 + nrm(ks[4], (DEPTH, D_MODEL), 0.02),
        "diff_lq1": nrm(ks[5], (DEPTH, HEAD_DIM), 0.1),
        "diff_lk1": nrm(ks[6], (DEPTH, HEAD_DIM), 0.1),
        "diff_lq2": nrm(ks[7], (DEPTH, HEAD_DIM), 0.1),
        "diff_lk2": nrm(ks[8], (DEPTH, HEAD_DIM), 0.1),
        "g_subln": 1.0 + nrm(ks[9], (DEPTH, DIFF_V_DIM), 0.02),
        "w_ple": nrm(ks[10], (DEPTH, PLE_DIM, D_MODEL), PLE_DIM ** -0.5),
        "w_ple_gate": nrm(ks[11], (DEPTH, D_MODEL, D_MODEL), D_MODEL ** -0.5),
        "g_ple": 1.0 + nrm(ks[12], (DEPTH, D_MODEL), 0.02),
        "g_final": 1.0 + nrm(ks[13], (D_MODEL,), 0.02),
    }


def reference(x, p, w_in, w_out, g_mix, diff_lq1, diff_lk1, diff_lq2, diff_lk2,
              g_subln, w_ple, w_ple_gate, g_ple, g_final):
    B, S, _ = x.shape
    cos, sin = rope_tables(S)
    offsets = [int(o) for o in np.cumsum(SPLIT_SIZES)[:-1]]
    h = x
    for i in range(DEPTH):
        lambda_init = 0.8 - 0.6 * float(np.exp(-0.3 * i))
        u = rmsnorm(h, g_mix[i])
        proj = jnp.einsum('bsd,de->bse', u, w_in[i])
        mq, mk, mv, dq, dk, dv, z = jnp.split(proj, offsets, axis=-1)
        to_heads = lambda t: t.reshape(B, S, MOBA_HEADS, HEAD_DIM).transpose(0, 2, 1, 3)
        m_out = moba_attention(to_heads(mq), to_heads(mk), to_heads(mv), cos, sin)
        m_out = m_out.transpose(0, 2, 1, 3).reshape(B, S, MOBA_W)
        dq = dq.reshape(B, S, DIFF_HEADS, 2, HEAD_DIM).transpose(0, 2, 3, 1, 4)
        dk = dk.reshape(B, S, DIFF_HEADS, 2, HEAD_DIM).transpose(0, 2, 3, 1, 4)
        dv = dv.reshape(B, S, DIFF_HEADS, DIFF_V_DIM).transpose(0, 2, 1, 3)
        lam = (jnp.exp(jnp.sum(diff_lq1[i].astype(jnp.float32) * diff_lk1[i].astype(jnp.float32)))
               - jnp.exp(jnp.sum(diff_lq2[i].astype(jnp.float32) * diff_lk2[i].astype(jnp.float32)))
               + lambda_init)
        d_out = diff_attention(dq, dk, dv, lam, g_subln[i], lambda_init, cos, sin)
        d_out = d_out.transpose(0, 2, 1, 3).reshape(B, S, DIFF_W)
        y = jnp.concatenate([m_out, d_out], axis=-1) * jax.nn.silu(z)
        h = h + jnp.einsum('bse,ed->bsd', y, w_out[i])
        pe = jnp.einsum('bsk,kd->bsd', p[i], w_ple[i])
        gate = jax.nn.sigmoid(jnp.einsum('bsd,de->bse', rmsnorm(h, g_ple[i]), w_ple_gate[i]))
        h = h + gate * pe
    return rmsnorm(h, g_final)
```

```python
import functools

import numpy as np
import jax
import jax.numpy as jnp
from jax import lax
from jax.experimental import pallas as pl
from jax.experimental.pallas import tpu as pltpu

HEAD_DIM = 64
MOBA_HEADS = 8
MOBA_BLOCK = 256
MOBA_TOPK = 3
DIFF_HEADS = 4
ROPE_THETA = 500000.0
ROPE_DIM = HEAD_DIM // 4
EPS = 1e-6

LANES = 128
MOBA_W = MOBA_HEADS * HEAD_DIM
DIFF_W = DIFF_HEADS * 2 * HEAD_DIM
SLAB = 512
QKV_W = 6 * SLAB
ROW_TILE = 512
Q_TILE = MOBA_BLOCK
VMEM_LIMIT_BYTES = 48 * 1024 * 1024

F32 = jnp.float32
BF16 = jnp.bfloat16
NEG_INF = float("-inf")


def _rope_tables(seq):
    inv = ROPE_THETA ** (-jnp.arange(0, ROPE_DIM, 2, dtype=F32) / ROPE_DIM)
    ang = jnp.arange(seq, dtype=F32)[:, None] * inv[None, :]
    cos, sin = jnp.cos(ang), jnp.sin(ang)
    half = ROPE_DIM // 2
    ones = jnp.ones((seq, HEAD_DIM - ROPE_DIM), F32)
    zeros = lambda n: jnp.zeros((seq, n), F32)
    c = jnp.concatenate([cos, cos, ones], axis=-1)
    s1 = jnp.concatenate([-sin, zeros(HEAD_DIM - half)], axis=-1)
    s2 = jnp.concatenate([zeros(half), sin, zeros(HEAD_DIM - ROPE_DIM)], axis=-1)
    k_tabs = [jnp.concatenate([t, t], axis=-1) for t in (c, s1, s2)]
    scale = HEAD_DIM ** -0.5
    q_tabs = [t * scale for t in k_tabs]
    return jnp.concatenate(q_tabs + k_tabs, axis=-1)


def _rmsnorm(x, g):
    return x * lax.rsqrt(jnp.mean(x * x, axis=-1, keepdims=True) + EPS) * g


def _inproj_kernel(x_ref, g_ref, w_ref, rope_ref, qkv_ref, z_ref, kmean_ref):
    tm = x_ref.shape[0]
    u = _rmsnorm(x_ref[...], g_ref[...]).astype(BF16)

    def rope(slab, tab0):
        outs = []
        for c in range(SLAB // LANES):
            t = slab[:, c * LANES:(c + 1) * LANES]
            cos = rope_ref[:, (tab0 + 0) * LANES:(tab0 + 1) * LANES]
            s1 = rope_ref[:, (tab0 + 1) * LANES:(tab0 + 2) * LANES]
            s2 = rope_ref[:, (tab0 + 2) * LANES:(tab0 + 3) * LANES]
            half = ROPE_DIM // 2
            outs.append(t * cos + pltpu.roll(t, LANES - half, 1) * s1 + pltpu.roll(t, half, 1) * s2)
        return jnp.concatenate(outs, axis=-1)

    for sec in range(6):
        slab = jnp.dot(u, w_ref[:, sec * SLAB:(sec + 1) * SLAB], preferred_element_type=F32)
        if sec in (0, 3):
            slab = rope(slab, 0)
        elif sec in (1, 4):
            slab = rope(slab, 3)
        if sec == 1:
            for r in range(tm // MOBA_BLOCK):
                kmean_ref[r] = jnp.mean(slab[r * MOBA_BLOCK:(r + 1) * MOBA_BLOCK], axis=0, keepdims=True)
        qkv_ref[:, sec * SLAB:(sec + 1) * SLAB] = slab.astype(BF16)
    z_ref[...] = jnp.dot(u, w_ref[:, QKV_W:], preferred_element_type=F32)


def _in_projection(x2d, g_mix, w_in_bf16, rope_tab, seq):
    rows, d_model = x2d.shape
    proj_w = w_in_bf16.shape[1]
    tm = ROW_TILE
    seq_tiles = seq // tm
    return pl.pallas_call(
        _inproj_kernel,
        grid=(rows // tm,),
        in_specs=[
            pl.BlockSpec((tm, d_model), lambda i: (i, 0)),
            pl.BlockSpec((1, d_model), lambda i: (0, 0)),
            pl.BlockSpec((d_model, proj_w), lambda i: (0, 0)),
            pl.BlockSpec((tm, 6 * LANES), lambda i: (i % seq_tiles, 0)),
        ],
        out_specs=[
            pl.BlockSpec((tm, QKV_W), lambda i: (i, 0)),
            pl.BlockSpec((tm, proj_w - QKV_W), lambda i: (i, 0)),
            pl.BlockSpec((tm // MOBA_BLOCK, 1, SLAB), lambda i: (i, 0, 0)),
        ],
        out_shape=[
            jax.ShapeDtypeStruct((rows, QKV_W), BF16),
            jax.ShapeDtypeStruct((rows, proj_w - QKV_W), F32),
            jax.ShapeDtypeStruct((rows // MOBA_BLOCK, 1, SLAB), F32),
        ],
        compiler_params=pltpu.CompilerParams(
            dimension_semantics=("arbitrary",), vmem_limit_bytes=VMEM_LIMIT_BYTES),
        name="in_projection",
    )(x2d, g_mix, w_in_bf16, rope_tab)


def _attn_kernel(*refs, moba, lambda_init):
    if moba:
        q_ref, k_ref, v_ref, kmean_ref, o_ref, acc_ref, m_ref, l_ref, bias_ref = refs
    else:
        (q_ref, k_ref, v_ref, lq1_ref, lk1_ref, lq2_ref, lk2_ref, gsub_ref,
         o_ref, acc_ref, m_ref, l_ref) = refs
    tq = q_ref.shape[1]
    tk = tq
    j = pl.program_id(2)

    q2 = q_ref[0]
    lane = lax.broadcasted_iota(jnp.int32, (tq, LANES), 1)
    first = lane < HEAD_DIM
    zero = jnp.zeros_like(q2)
    qs = (jnp.where(first, q2, zero), jnp.where(first, zero, q2))
    nt = (((1,), (1,)), ((), ()))

    if moba:
        km = kmean_ref[0]
        km_hi = km.astype(BF16)
        km_lo = (km - km_hi.astype(F32)).astype(BF16)
        nb = km.shape[0]
        blk = lax.broadcasted_iota(jnp.int32, (nb, tq), 0)
        for h in range(2):
            gate = (lax.dot_general(km_hi, qs[h], nt, preferred_element_type=F32)
                    + lax.dot_general(km_lo, qs[h], nt, preferred_element_type=F32))
            past = blk < j
            gate = jnp.where(past, gate, NEG_INF)
            beaten = jnp.zeros((nb, tq), jnp.int32)
            for m in range(nb):
                gm = gate[m:m + 1, :]
                beats = (gm > gate) | ((gm == gate) & (blk > m))
                beaten = beaten + beats.astype(jnp.int32)
            chosen = past & (beaten < MOBA_TOPK)
            bias_t = jnp.where(chosen, 0.0, NEG_INF).astype(F32)
            bias_t = jnp.concatenate([bias_t, jnp.zeros((LANES - nb, tq), F32)], axis=0)
            bias_ref[h] = bias_t.T

    def update(h, s, vblk, first_block):
        m_cur = jnp.max(s, axis=1, keepdims=True)
        if first_block:
            m_next = jnp.broadcast_to(m_cur, (tq, LANES))
        else:
            m_prev = m_ref[h]
            m_next = jnp.maximum(m_prev, m_cur)
            alpha = jnp.exp(m_prev - m_next)
        p = jnp.exp(s - pltpu.repeat(m_next, tk // LANES, axis=1))
        l_cur = jnp.sum(p, axis=1, keepdims=True)
        pv = jnp.dot(p.astype(BF16), vblk, preferred_element_type=F32)
        if first_block:
            l_ref[h] = jnp.broadcast_to(l_cur, (tq, LANES))
            acc_ref[h] = pv
        else:
            l_ref[h] = alpha * l_ref[h] + l_cur
            acc_ref[h] = alpha * acc_ref[h] + pv
        m_ref[h] = m_next

    own = pl.multiple_of(j * tq, tq)
    kd = k_ref[0, pl.ds(own, tk), :]
    vd = v_ref[0, pl.ds(own, tk), :]
    causal = (lax.broadcasted_iota(jnp.int32, (tq, tk), 1)
              <= lax.broadcasted_iota(jnp.int32, (tq, tk), 0))
    for h in range(2):
        s = lax.dot_general(qs[h], kd, nt, preferred_element_type=F32)
        update(h, jnp.where(causal, s, NEG_INF), vd, True)

    def past_block(n, carry):
        off = pl.multiple_of(n * tk, tk)
        kb = k_ref[0, pl.ds(off, tk), :]
        vb = v_ref[0, pl.ds(off, tk), :]
        for h in range(2):
            s = lax.dot_general(qs[h], kb, nt, preferred_element_type=F32)
            if moba:
                s = s + jnp.sum(jnp.where(lane == n, bias_ref[h], 0.0), axis=1, keepdims=True)
            update(h, s, vb, False)
        return carry

    lax.fori_loop(0, j, past_block, 0)

    o0 = acc_ref[0] / l_ref[0]
    o1 = acc_ref[1] / l_ref[1]
    if moba:
        o_ref[0] = jnp.where(first, o0, o1)
    else:
        lam = (jnp.exp(jnp.sum(lq1_ref[...] * lk1_ref[...], axis=1, keepdims=True))
               - jnp.exp(jnp.sum(lq2_ref[...] * lk2_ref[...], axis=1, keepdims=True))
               + lambda_init)
        o = o0 - lam * o1
        o_ref[0] = _rmsnorm(o, gsub_ref[...]) * (1.0 - lambda_init)


def _attention(qkv, extra, *, moba, lambda_init, q_col, k_col, v_col, groups):
    batch, seq, _ = qkv.shape
    tq = Q_TILE
    in_specs = [
        pl.BlockSpec((1, tq, LANES), lambda b, g, j: (b, j, q_col + g)),
        pl.BlockSpec((1, seq, LANES), lambda b, g, j: (b, 0, k_col + g)),
        pl.BlockSpec((1, seq, LANES), lambda b, g, j: (b, 0, v_col + g)),
    ]
    scratch = [
        pltpu.VMEM((2, tq, LANES), F32),
        pltpu.VMEM((2, tq, LANES), F32),
        pltpu.VMEM((2, tq, LANES), F32),
    ]
    if moba:
        (kmean,) = extra
        in_specs.append(pl.BlockSpec((1, kmean.shape[1], LANES), lambda b, g, j: (b, 0, g)))
        scratch.append(pltpu.VMEM((2, tq, LANES), F32))
    else:
        for a in extra:
            in_specs.append(pl.BlockSpec(a.shape, lambda b, g, j: (0, 0)))
    return pl.pallas_call(
        functools.partial(_attn_kernel, moba=moba, lambda_init=lambda_init),
        grid=(batch, groups, seq // tq),
        in_specs=in_specs,
        out_specs=pl.BlockSpec((1, tq, LANES), lambda b, g, j: (b, j, g)),
        out_shape=jax.ShapeDtypeStruct((batch, seq, groups * LANES), F32),
        scratch_shapes=scratch,
        compiler_params=pltpu.CompilerParams(
            dimension_semantics=("arbitrary", "arbitrary", "arbitrary"),
            vmem_limit_bytes=VMEM_LIMIT_BYTES),
        name="moba_attention" if moba else "diff_attention",
    )(qkv, qkv, qkv, *extra)


def _outproj_kernel(x_ref, m_ref, d_ref, z_ref, p_ref, wout_ref, wple_ref, wgate_ref,
                    gple_ref, gfin_ref, o_ref, *, final):
    mw = m_ref.shape[1]
    z = z_ref[...]
    gz = z * jax.nn.sigmoid(z)
    ym = (m_ref[...] * gz[:, :mw]).astype(BF16)
    yd = (d_ref[...] * gz[:, mw:]).astype(BF16)
    h = (x_ref[...]
         + jnp.dot(ym, wout_ref[:mw, :], preferred_element_type=F32)
         + jnp.dot(yd, wout_ref[mw:, :], preferred_element_type=F32))
    pe = jnp.dot(p_ref[...].astype(BF16), wple_ref[...], preferred_element_type=F32)
    hn = _rmsnorm(h, gple_ref[...]).astype(BF16)
    gate = jax.nn.sigmoid(jnp.dot(hn, wgate_ref[...], preferred_element_type=F32))
    h = h + gate * pe
    o_ref[...] = _rmsnorm(h, gfin_ref[...]) if final else h


def _out_projection(x2d, m_out, d_out, z, p2d, w_out, w_ple, w_gate, g_ple, g_final, final):
    rows, d_model = x2d.shape
    tm = ROW_TILE
    row_spec = lambda a: pl.BlockSpec((tm, a.shape[1]), lambda i: (i, 0))
    full_spec = lambda a: pl.BlockSpec(a.shape, lambda i: (0, 0))
    return pl.pallas_call(
        functools.partial(_outproj_kernel, final=final),
        grid=(rows // tm,),
        in_specs=[row_spec(x2d), row_spec(m_out), row_spec(d_out), row_spec(z), row_spec(p2d),
                  full_spec(w_out), full_spec(w_ple), full_spec(w_gate),
                  full_spec(g_ple), full_spec(g_final)],
        out_specs=pl.BlockSpec((tm, d_model), lambda i: (i, 0)),
        out_shape=jax.ShapeDtypeStruct((rows, d_model), F32),
        compiler_params=pltpu.CompilerParams(
            dimension_semantics=("arbitrary",), vmem_limit_bytes=VMEM_LIMIT_BYTES),
        name="out_projection",
    )(x2d, m_out, d_out, z, p2d, w_out, w_ple, w_gate, g_ple, g_final)


def kernel(x, p, w_in, w_out, g_mix, diff_lq1, diff_lk1, diff_lq2, diff_lk2, g_subln,
           w_ple, w_ple_gate, g_ple, g_final):
    batch, seq, d_model = x.shape
    depth = w_in.shape[0]
    rows = batch * seq
    assert w_in.shape[2] == QKV_W + MOBA_W + DIFF_W
    assert seq % ROW_TILE == 0 and ROW_TILE % MOBA_BLOCK == 0 and Q_TILE == MOBA_BLOCK
    rope_tab = _rope_tables(seq)
    row = lambda a: a.reshape(1, -1)
    h = x.reshape(rows, d_model)
    for i in range(depth):
        lambda_init = 0.8 - 0.6 * float(np.exp(-0.3 * i))
        qkv, z, kmean = _in_projection(h, row(g_mix[i]), w_in[i].astype(BF16), rope_tab, seq)
        qkv = qkv.reshape(batch, seq, QKV_W)
        kmean = kmean.reshape(batch, seq // MOBA_BLOCK, SLAB)
        sec = SLAB // LANES
        m_out = _attention(qkv, (kmean,), moba=True, lambda_init=lambda_init,
                           q_col=0, k_col=sec, v_col=2 * sec, groups=MOBA_W // LANES)
        d_out = _attention(qkv, (row(diff_lq1[i]), row(diff_lk1[i]), row(diff_lq2[i]),
                                 row(diff_lk2[i]), row(g_subln[i])),
                           moba=False, lambda_init=lambda_init,
                           q_col=3 * sec, k_col=4 * sec, v_col=5 * sec, groups=DIFF_W // LANES)
        h = _out_projection(h, m_out.reshape(rows, MOBA_W), d_out.reshape(rows, DIFF_W), z,
                            p[i].reshape(rows, -1), w_out[i].astype(BF16),
                            w_ple[i].astype(BF16), w_ple_gate[i].astype(BF16),
                            row(g_ple[i]), row(g_final), final=(i == depth - 1))
    return h.reshape(batch, seq, d_model)
```

```python
import functools

import numpy as np
import jax
import jax.numpy as jnp
from jax import lax
from jax.experimental import pallas as pl
from jax.experimental.pallas import tpu as pltpu

HEAD_DIM = 64
MOBA_HEADS = 8
MOBA_BLOCK = 256
MOBA_TOPK = 3
DIFF_HEADS = 4
ROPE_THETA = 500000.0
ROPE_DIM = HEAD_DIM // 4
EPS = 1e-6

LANES = 128
MOBA_W = MOBA_HEADS * HEAD_DIM
DIFF_W = DIFF_HEADS * 2 * HEAD_DIM
SLAB = 512
QKV_W = 6 * SLAB
ROW_TILE = 512
Q_TILE = MOBA_BLOCK
VMEM_LIMIT_BYTES = 48 * 1024 * 1024

F32 = jnp.float32
BF16 = jnp.bfloat16
NEG_INF = float("-inf")
MASK_BIAS = -(2.0 ** 100)


def _rope_tables(seq):
    inv = ROPE_THETA ** (-jnp.arange(0, ROPE_DIM, 2, dtype=F32) / ROPE_DIM)
    ang = jnp.arange(seq, dtype=F32)[:, None] * inv[None, :]
    cos, sin = jnp.cos(ang), jnp.sin(ang)
    half = ROPE_DIM // 2
    ones = jnp.ones((seq, HEAD_DIM - ROPE_DIM), F32)
    zeros = lambda n: jnp.zeros((seq, n), F32)
    c = jnp.concatenate([cos, cos, ones], axis=-1)
    s1 = jnp.concatenate([-sin, zeros(HEAD_DIM - half)], axis=-1)
    s2 = jnp.concatenate([zeros(half), sin, zeros(HEAD_DIM - ROPE_DIM)], axis=-1)
    k_tabs = [jnp.concatenate([t, t], axis=-1) for t in (c, s1, s2)]
    scale = HEAD_DIM ** -0.5
    q_tabs = [t * scale for t in k_tabs]
    return jnp.concatenate(q_tabs + k_tabs, axis=-1)


def _rmsnorm(x, g):
    return x * lax.rsqrt(jnp.mean(x * x, axis=-1, keepdims=True) + EPS) * g


def _inproj_kernel(x_ref, g_ref, w_ref, rope_ref, qkv_ref, z_ref, kmean_ref):
    tm = x_ref.shape[0]
    u = _rmsnorm(x_ref[...], g_ref[...]).astype(BF16)

    def rope(slab, tab0):
        outs = []
        for c in range(SLAB // LANES):
            t = slab[:, c * LANES:(c + 1) * LANES]
            cos = rope_ref[:, (tab0 + 0) * LANES:(tab0 + 1) * LANES]
            s1 = rope_ref[:, (tab0 + 1) * LANES:(tab0 + 2) * LANES]
            s2 = rope_ref[:, (tab0 + 2) * LANES:(tab0 + 3) * LANES]
            half = ROPE_DIM // 2
            outs.append(t * cos + pltpu.roll(t, LANES - half, 1) * s1 + pltpu.roll(t, half, 1) * s2)
        return jnp.concatenate(outs, axis=-1)

    for sec in range(6):
        slab = jnp.dot(u, w_ref[:, sec * SLAB:(sec + 1) * SLAB], preferred_element_type=F32)
        if sec in (0, 3):
            slab = rope(slab, 0)
        elif sec in (1, 4):
            slab = rope(slab, 3)
        if sec == 1:
            for r in range(tm // MOBA_BLOCK):
                kmean_ref[r] = jnp.mean(slab[r * MOBA_BLOCK:(r + 1) * MOBA_BLOCK], axis=0, keepdims=True)
        qkv_ref[:, sec * SLAB:(sec + 1) * SLAB] = slab.astype(BF16)
    z_ref[...] = jnp.dot(u, w_ref[:, QKV_W:], preferred_element_type=F32)


def _in_projection(x2d, g_mix, w_in_bf16, rope_tab, seq):
    rows, d_model = x2d.shape
    proj_w = w_in_bf16.shape[1]
    tm = ROW_TILE
    seq_tiles = seq // tm
    return pl.pallas_call(
        _inproj_kernel,
        grid=(rows // tm,),
        in_specs=[
            pl.BlockSpec((tm, d_model), lambda i: (i, 0)),
            pl.BlockSpec((1, d_model), lambda i: (0, 0)),
            pl.BlockSpec((d_model, proj_w), lambda i: (0, 0)),
            pl.BlockSpec((tm, 6 * LANES), lambda i: (i % seq_tiles, 0)),
        ],
        out_specs=[
            pl.BlockSpec((tm, QKV_W), lambda i: (i, 0)),
            pl.BlockSpec((tm, proj_w - QKV_W), lambda i: (i, 0)),
            pl.BlockSpec((tm // MOBA_BLOCK, 1, SLAB), lambda i: (i, 0, 0)),
        ],
        out_shape=[
            jax.ShapeDtypeStruct((rows, QKV_W), BF16),
            jax.ShapeDtypeStruct((rows, proj_w - QKV_W), F32),
            jax.ShapeDtypeStruct((rows // MOBA_BLOCK, 1, SLAB), F32),
        ],
        compiler_params=pltpu.CompilerParams(
            dimension_semantics=("arbitrary",), vmem_limit_bytes=VMEM_LIMIT_BYTES),
        name="in_projection",
    )(x2d, g_mix, w_in_bf16, rope_tab)


def _attn_kernel(*refs, moba, lambda_init):
    if moba:
        q_ref, k_ref, v_ref, kmean_ref, o_ref = refs
    else:
        q_ref, k_ref, v_ref, lq1_ref, lk1_ref, lq2_ref, lk2_ref, gsub_ref, o_ref = refs
    seq = q_ref.shape[1]
    tq = Q_TILE
    nt = (((1,), (1,)), ((), ()))

    first_q = lax.broadcasted_iota(jnp.int32, (tq, LANES), 1) < HEAD_DIM
    own_lanes = (first_q, jnp.logical_not(first_q))
    causal = (lax.broadcasted_iota(jnp.int32, (tq, tq), 1)
              <= lax.broadcasted_iota(jnp.int32, (tq, tq), 0))
    k2 = k_ref[0]
    v2 = v_ref[0]

    if moba:
        lane_k = lax.broadcasted_iota(jnp.int32, (seq, LANES), 1)
        key_blk = lax.broadcasted_iota(jnp.int32, (seq, LANES), 0) // MOBA_BLOCK
        first_k = lane_k < HEAD_DIM
        hot_hi = jnp.where(lane_k - HEAD_DIM == key_blk, 1.0, 0.0).astype(BF16)
        hot_lo = jnp.where(lane_k == key_blk, 1.0, 0.0).astype(BF16)
        ks = (jnp.where(first_k, k2, hot_hi), jnp.where(first_k, hot_lo, k2))
        km = kmean_ref[0]
        km_hi = km.astype(BF16)
        km_lo = (km - km_hi.astype(F32)).astype(BF16)
        nb = km.shape[0]
        blk = lax.broadcasted_iota(jnp.int32, (nb, tq), 0)
    else:
        ks = (k2, k2)
        lam = (jnp.exp(jnp.sum(lq1_ref[...] * lk1_ref[...], axis=1, keepdims=True))
               - jnp.exp(jnp.sum(lq2_ref[...] * lk2_ref[...], axis=1, keepdims=True))
               + lambda_init)

    def block_mask(q2, j):
        rows = []
        for h in (1, 0):
            qh = jnp.where(own_lanes[h], q2, jnp.zeros_like(q2))
            gate = (lax.dot_general(km_hi, qh, nt, preferred_element_type=F32)
                    + lax.dot_general(km_lo, qh, nt, preferred_element_type=F32))
            past = blk < j
            gate = jnp.where(past, gate, NEG_INF)
            beaten = jnp.zeros((nb, tq), jnp.int32)
            for m in range(j):
                gm = gate[m:m + 1, :]
                beats = (gm > gate) | ((gm == gate) & (blk > m))
                beaten = beaten + beats.astype(jnp.int32)
            dropped = past & (beaten >= MOBA_TOPK)
            rows.append(jnp.where(dropped, MASK_BIAS, 0.0).astype(F32))
            rows.append(jnp.zeros((HEAD_DIM - nb, tq), F32))
        return jnp.concatenate(rows, axis=0).T.astype(BF16)

    for j in range(seq // tq):
        length = (j + 1) * tq
        q2 = q_ref[0, j * tq:(j + 1) * tq, :]
        other = block_mask(q2, j) if (moba and j > MOBA_TOPK) else jnp.zeros_like(q2)
        outs = []
        for h in range(2):
            qa = jnp.where(own_lanes[h], q2, other)
            s = lax.dot_general(qa, ks[h][:length], nt, preferred_element_type=F32)
            s_own = jnp.where(causal, s[:, length - tq:], NEG_INF)
            m = jnp.max(s_own, axis=1, keepdims=True)
            if j > 0:
                s_past = s[:, :length - tq]
                m = jnp.maximum(m, jnp.max(s_past, axis=1, keepdims=True))
            p = jnp.exp(s_own - m)
            l = jnp.sum(p, axis=1, keepdims=True)
            if j > 0:
                p_past = jnp.exp(s_past - m)
                l = l + jnp.sum(p_past, axis=1, keepdims=True)
                p = jnp.concatenate([p_past, p], axis=1)
            acc = jnp.dot(p.astype(BF16), v2[:length], preferred_element_type=F32)
            outs.append(acc / l)
        if moba:
            out = jnp.where(first_q, outs[0], outs[1])
        else:
            out = _rmsnorm(outs[0] - lam * outs[1], gsub_ref[...]) * (1.0 - lambda_init)
        o_ref[0, j * tq:(j + 1) * tq, :] = out


def _attention(qkv, extra, *, moba, lambda_init, q_col, k_col, v_col, groups):
    batch, seq, _ = qkv.shape
    col_spec = lambda col: pl.BlockSpec((1, seq, LANES), lambda b, g: (b, 0, col + g))
    in_specs = [col_spec(q_col), col_spec(k_col), col_spec(v_col)]
    if moba:
        (kmean,) = extra
        in_specs.append(pl.BlockSpec((1, kmean.shape[1], LANES), lambda b, g: (b, 0, g)))
    else:
        for a in extra:
            in_specs.append(pl.BlockSpec(a.shape, lambda b, g: (0, 0)))
    return pl.pallas_call(
        functools.partial(_attn_kernel, moba=moba, lambda_init=lambda_init),
        grid=(batch, groups),
        in_specs=in_specs,
        out_specs=pl.BlockSpec((1, seq, LANES), lambda b, g: (b, 0, g)),
        out_shape=jax.ShapeDtypeStruct((batch, seq, groups * LANES), F32),
        compiler_params=pltpu.CompilerParams(
            dimension_semantics=("arbitrary", "arbitrary"),
            vmem_limit_bytes=VMEM_LIMIT_BYTES),
        name="moba_attention" if moba else "diff_attention",
    )(qkv, qkv, qkv, *extra)


def _outproj_kernel(x_ref, m_ref, d_ref, z_ref, p_ref, wout_ref, wple_ref, wgate_ref,
                    gple_ref, gfin_ref, o_ref, *, final):
    mw = m_ref.shape[1]
    z = z_ref[...]
    gz = z * jax.nn.sigmoid(z)
    ym = (m_ref[...] * gz[:, :mw]).astype(BF16)
    yd = (d_ref[...] * gz[:, mw:]).astype(BF16)
    h = (x_ref[...]
         + jnp.dot(ym, wout_ref[:mw, :], preferred_element_type=F32)
         + jnp.dot(yd, wout_ref[mw:, :], preferred_element_type=F32))
    pe = jnp.dot(p_ref[...].astype(BF16), wple_ref[...], preferred_element_type=F32)
    hn = _rmsnorm(h, gple_ref[...]).astype(BF16)
    gate = jax.nn.sigmoid(jnp.dot(hn, wgate_ref[...], preferred_element_type=F32))
    h = h + gate * pe
    o_ref[...] = _rmsnorm(h, gfin_ref[...]) if final else h


def _out_projection(x2d, m_out, d_out, z, p2d, w_out, w_ple, w_gate, g_ple, g_final, final):
    rows, d_model = x2d.shape
    tm = ROW_TILE
    row_spec = lambda a: pl.BlockSpec((tm, a.shape[1]), lambda i: (i, 0))
    full_spec = lambda a: pl.BlockSpec(a.shape, lambda i: (0, 0))
    return pl.pallas_call(
        functools.partial(_outproj_kernel, final=final),
        grid=(rows // tm,),
        in_specs=[row_spec(x2d), row_spec(m_out), row_spec(d_out), row_spec(z), row_spec(p2d),
                  full_spec(w_out), full_spec(w_ple), full_spec(w_gate),
                  full_spec(g_ple), full_spec(g_final)],
        out_specs=pl.BlockSpec((tm, d_model), lambda i: (i, 0)),
        out_shape=jax.ShapeDtypeStruct((rows, d_model), F32),
        compiler_params=pltpu.CompilerParams(
            dimension_semantics=("arbitrary",), vmem_limit_bytes=VMEM_LIMIT_BYTES),
        name="out_projection",
    )(x2d, m_out, d_out, z, p2d, w_out, w_ple, w_gate, g_ple, g_final)


def kernel(x, p, w_in, w_out, g_mix, diff_lq1, diff_lk1, diff_lq2, diff_lk2, g_subln,
           w_ple, w_ple_gate, g_ple, g_final):
    batch, seq, d_model = x.shape
    depth = w_in.shape[0]
    rows = batch * seq
    assert w_in.shape[2] == QKV_W + MOBA_W + DIFF_W
    assert seq % ROW_TILE == 0 and ROW_TILE % MOBA_BLOCK == 0 and Q_TILE == MOBA_BLOCK
    rope_tab = _rope_tables(seq)
    row = lambda a: a.reshape(1, -1)
    h = x.reshape(rows, d_model)
    for i in range(depth):
        lambda_init = 0.8 - 0.6 * float(np.exp(-0.3 * i))
        qkv, z, kmean = _in_projection(h, row(g_mix[i]), w_in[i].astype(BF16), rope_tab, seq)
        qkv = qkv.reshape(batch, seq, QKV_W)
        kmean = kmean.reshape(batch, seq // MOBA_BLOCK, SLAB)
        sec = SLAB // LANES
        m_out = _attention(qkv, (kmean,), moba=True, lambda_init=lambda_init,
                           q_col=0, k_col=sec, v_col=2 * sec, groups=MOBA_W // LANES)
        d_out = _attention(qkv, (row(diff_lq1[i]), row(diff_lk1[i]), row(diff_lq2[i]),
                                 row(diff_lk2[i]), row(g_subln[i])),
                           moba=False, lambda_init=lambda_init,
                           q_col=3 * sec, k_col=4 * sec, v_col=5 * sec, groups=DIFF_W // LANES)
        h = _out_projection(h, m_out.reshape(rows, MOBA_W), d_out.reshape(rows, DIFF_W), z,
                            p[i].reshape(rows, -1), w_out[i].astype(BF16),
                            w_ple[i].astype(BF16), w_ple_gate[i].astype(BF16),
                            row(g_ple[i]), row(g_final), final=(i == depth - 1))
    return h.reshape(batch, seq, d_model)
```

```python
import functools

import numpy as np
import jax
import jax.numpy as jnp
from jax import lax
from jax.experimental import pallas as pl
from jax.experimental.pallas import tpu as pltpu

HEAD_DIM = 64
MOBA_HEADS = 8
MOBA_BLOCK = 256
MOBA_TOPK = 3
DIFF_HEADS = 4
ROPE_THETA = 500000.0
ROPE_DIM = HEAD_DIM // 4
EPS = 1e-6

LANES = 128
MOBA_W = MOBA_HEADS * HEAD_DIM
DIFF_W = DIFF_HEADS * 2 * HEAD_DIM
SLAB = 512
QKV_W = 6 * SLAB
ROW_TILE = 512
Q_TILE = MOBA_BLOCK
VMEM_LIMIT_BYTES = 48 * 1024 * 1024

F32 = jnp.float32
BF16 = jnp.bfloat16
NEG_INF = float("-inf")
MASK_BIAS = -(2.0 ** 100)


def _rope_tables(seq):
    inv = ROPE_THETA ** (-jnp.arange(0, ROPE_DIM, 2, dtype=F32) / ROPE_DIM)
    ang = jnp.arange(seq, dtype=F32)[:, None] * inv[None, :]
    cos, sin = jnp.cos(ang), jnp.sin(ang)
    half = ROPE_DIM // 2
    ones = jnp.ones((seq, HEAD_DIM - ROPE_DIM), F32)
    zeros = lambda n: jnp.zeros((seq, n), F32)
    c = jnp.concatenate([cos, cos, ones], axis=-1)
    s1 = jnp.concatenate([-sin, zeros(HEAD_DIM - half)], axis=-1)
    s2 = jnp.concatenate([zeros(half), sin, zeros(HEAD_DIM - ROPE_DIM)], axis=-1)
    k_tabs = [jnp.concatenate([t, t], axis=-1) for t in (c, s1, s2)]
    scale = HEAD_DIM ** -0.5 * float(np.log2(np.e))
    q_tabs = [t * scale for t in k_tabs]
    return jnp.concatenate(q_tabs + k_tabs, axis=-1)


def _rmsnorm(x, g):
    return x * lax.rsqrt(jnp.mean(x * x, axis=-1, keepdims=True) + EPS) * g


def _inproj_kernel(x_ref, g_ref, w_ref, rope_ref, qkv_ref, z_ref, kmean_ref):
    tm = x_ref.shape[0]
    u = _rmsnorm(x_ref[...], g_ref[...]).astype(BF16)

    def rope(slab, tab0):
        outs = []
        for c in range(SLAB // LANES):
            t = slab[:, c * LANES:(c + 1) * LANES]
            cos = rope_ref[:, (tab0 + 0) * LANES:(tab0 + 1) * LANES]
            s1 = rope_ref[:, (tab0 + 1) * LANES:(tab0 + 2) * LANES]
            s2 = rope_ref[:, (tab0 + 2) * LANES:(tab0 + 3) * LANES]
            half = ROPE_DIM // 2
            outs.append(t * cos + pltpu.roll(t, LANES - half, 1) * s1 + pltpu.roll(t, half, 1) * s2)
        return jnp.concatenate(outs, axis=-1)

    for sec in range(6):
        slab = jnp.dot(u, w_ref[:, sec * SLAB:(sec + 1) * SLAB], preferred_element_type=F32)
        if sec in (0, 3):
            slab = rope(slab, 0)
        elif sec in (1, 4):
            slab = rope(slab, 3)
        if sec == 1:
            for r in range(tm // MOBA_BLOCK):
                kmean_ref[r] = jnp.mean(slab[r * MOBA_BLOCK:(r + 1) * MOBA_BLOCK], axis=0, keepdims=True)
        qkv_ref[:, sec * SLAB:(sec + 1) * SLAB] = slab.astype(BF16)
    z_ref[...] = jnp.dot(u, w_ref[:, QKV_W:], preferred_element_type=F32)


def _in_projection(x2d, g_mix, w_in_bf16, rope_tab, seq):
    rows, d_model = x2d.shape
    proj_w = w_in_bf16.shape[1]
    tm = ROW_TILE
    seq_tiles = seq // tm
    return pl.pallas_call(
        _inproj_kernel,
        grid=(rows // tm,),
        in_specs=[
            pl.BlockSpec((tm, d_model), lambda i: (i, 0)),
            pl.BlockSpec((1, d_model), lambda i: (0, 0)),
            pl.BlockSpec((d_model, proj_w), lambda i: (0, 0)),
            pl.BlockSpec((tm, 6 * LANES), lambda i: (i % seq_tiles, 0)),
        ],
        out_specs=[
            pl.BlockSpec((tm, QKV_W), lambda i: (i, 0)),
            pl.BlockSpec((tm, proj_w - QKV_W), lambda i: (i, 0)),
            pl.BlockSpec((tm // MOBA_BLOCK, 1, SLAB), lambda i: (i, 0, 0)),
        ],
        out_shape=[
            jax.ShapeDtypeStruct((rows, QKV_W), BF16),
            jax.ShapeDtypeStruct((rows, proj_w - QKV_W), F32),
            jax.ShapeDtypeStruct((rows // MOBA_BLOCK, 1, SLAB), F32),
        ],
        compiler_params=pltpu.CompilerParams(
            dimension_semantics=("arbitrary",), vmem_limit_bytes=VMEM_LIMIT_BYTES),
        name="in_projection",
    )(x2d, g_mix, w_in_bf16, rope_tab)


def _attn_kernel(*refs, moba, lambda_init):
    if moba:
        q_ref, k_ref, v_ref, kmean_ref, o_ref = refs
    else:
        q_ref, k_ref, v_ref, lq1_ref, lk1_ref, lq2_ref, lk2_ref, gsub_ref, o_ref = refs
    seq = q_ref.shape[1]
    tq = Q_TILE
    nt = (((1,), (1,)), ((), ()))

    first_q = lax.broadcasted_iota(jnp.int32, (tq, LANES), 1) < HEAD_DIM
    own_lanes = (first_q, jnp.logical_not(first_q))
    causal = (lax.broadcasted_iota(jnp.int32, (tq, tq), 1)
              <= lax.broadcasted_iota(jnp.int32, (tq, tq), 0))
    k2 = k_ref[0]
    v2 = v_ref[0]
    v_ones = jnp.concatenate([v2, jnp.ones_like(v2)], axis=1)

    if moba:
        lane_k = lax.broadcasted_iota(jnp.int32, (seq, LANES), 1)
        key_blk = lax.broadcasted_iota(jnp.int32, (seq, LANES), 0) // MOBA_BLOCK
        first_k = lane_k < HEAD_DIM
        hot_hi = jnp.where(lane_k - HEAD_DIM == key_blk, 1.0, 0.0).astype(BF16)
        hot_lo = jnp.where(lane_k == key_blk, 1.0, 0.0).astype(BF16)
        ks = (jnp.where(first_k, k2, hot_hi), jnp.where(first_k, hot_lo, k2))
        km = kmean_ref[0]
        km_hi = km.astype(BF16)
        km_lo = (km - km_hi.astype(F32)).astype(BF16)
        nb = km.shape[0]
        blk = lax.broadcasted_iota(jnp.int32, (nb, tq), 0)
    else:
        ks = (k2, k2)
        lam = (jnp.exp(jnp.sum(lq1_ref[...] * lk1_ref[...], axis=1, keepdims=True))
               - jnp.exp(jnp.sum(lq2_ref[...] * lk2_ref[...], axis=1, keepdims=True))
               + lambda_init)

    def block_mask(q2, j):
        rows = []
        for h in (1, 0):
            qh = jnp.where(own_lanes[h], q2, jnp.zeros_like(q2))
            gate = (lax.dot_general(km_hi, qh, nt, preferred_element_type=F32)
                    + lax.dot_general(km_lo, qh, nt, preferred_element_type=F32))
            past = blk < j
            gate = jnp.where(past, gate, NEG_INF)
            beaten = jnp.zeros((nb, tq), jnp.int32)
            for m in range(j):
                gm = gate[m:m + 1, :]
                beats = (gm > gate) | ((gm == gate) & (blk > m))
                beaten = beaten + beats.astype(jnp.int32)
            dropped = past & (beaten >= MOBA_TOPK)
            rows.append(jnp.where(dropped, MASK_BIAS, 0.0).astype(F32))
            rows.append(jnp.zeros((HEAD_DIM - nb, tq), F32))
        return jnp.concatenate(rows, axis=0).T.astype(BF16)

    def query_side(j):
        q2 = q_ref[0, j * tq:(j + 1) * tq, :]
        other = block_mask(q2, j) if (moba and j > MOBA_TOPK) else jnp.zeros_like(q2)
        return [jnp.where(own_lanes[h], q2, other) for h in range(2)]

    def scores(qa, j, h):
        return lax.dot_general(qa, ks[h][:(j + 1) * tq], nt, preferred_element_type=F32)

    def weighted_values(s, j):
        length = (j + 1) * tq
        s_own = jnp.where(causal, s[:, length - tq:], NEG_INF)
        m = jnp.max(s_own, axis=1, keepdims=True)
        if j > 0:
            s_past = s[:, :length - tq]
            m = jnp.maximum(m, jnp.max(s_past, axis=1, keepdims=True))
        p = jnp.exp2(s_own - m)
        if j > 0:
            p = jnp.concatenate([jnp.exp2(s_past - m), p], axis=1)
        acc = jnp.dot(p.astype(BF16), v_ones[:length], preferred_element_type=F32)
        return acc[:, :LANES] / acc[:, LANES:]

    units = [(j, h) for j in reversed(range(seq // tq)) for h in range(2)]
    qa = query_side(units[0][0])
    s_next = scores(qa[0], *units[0])
    outs = []
    for idx, (j, h) in enumerate(units):
        s = s_next
        if idx + 1 < len(units):
            jn, hn = units[idx + 1]
            if hn == 0:
                qa = query_side(jn)
            s_next = scores(qa[hn], jn, hn)
        outs.append(weighted_values(s, j))
        if h == 1:
            if moba:
                out = jnp.where(first_q, outs[0], outs[1])
            else:
                out = _rmsnorm(outs[0] - lam * outs[1], gsub_ref[...]) * (1.0 - lambda_init)
            o_ref[0, j * tq:(j + 1) * tq, :] = out
            outs = []


def _attention(qkv, extra, *, moba, lambda_init, q_col, k_col, v_col, groups):
    batch, seq, _ = qkv.shape
    col_spec = lambda col: pl.BlockSpec((1, seq, LANES), lambda b, g: (b, 0, col + g))
    in_specs = [col_spec(q_col), col_spec(k_col), col_spec(v_col)]
    if moba:
        (kmean,) = extra
        in_specs.append(pl.BlockSpec((1, kmean.shape[1], LANES), lambda b, g: (b, 0, g)))
    else:
        for a in extra:
            in_specs.append(pl.BlockSpec(a.shape, lambda b, g: (0, 0)))
    return pl.pallas_call(
        functools.partial(_attn_kernel, moba=moba, lambda_init=lambda_init),
        grid=(batch, groups),
        in_specs=in_specs,
        out_specs=pl.BlockSpec((1, seq, LANES), lambda b, g: (b, 0, g)),
        out_shape=jax.ShapeDtypeStruct((batch, seq, groups * LANES), F32),
        compiler_params=pltpu.CompilerParams(
            dimension_semantics=("arbitrary", "arbitrary"),
            vmem_limit_bytes=VMEM_LIMIT_BYTES),
        name="moba_attention" if moba else "diff_attention",
    )(qkv, qkv, qkv, *extra)


def _outproj_kernel(x_ref, m_ref, d_ref, z_ref, p_ref, wout_ref, wple_ref, wgate_ref,
                    gple_ref, gfin_ref, o_ref, *, final):
    mw = m_ref.shape[1]
    z = z_ref[...]
    gz = z * jax.nn.sigmoid(z)
    ym = (m_ref[...] * gz[:, :mw]).astype(BF16)
    yd = (d_ref[...] * gz[:, mw:]).astype(BF16)
    h = (x_ref[...]
         + jnp.dot(ym, wout_ref[:mw, :], preferred_element_type=F32)
         + jnp.dot(yd, wout_ref[mw:, :], preferred_element_type=F32))
    pe = jnp.dot(p_ref[...].astype(BF16), wple_ref[...], preferred_element_type=F32)
    hn = _rmsnorm(h, gple_ref[...]).astype(BF16)
    gate = jax.nn.sigmoid(jnp.dot(hn, wgate_ref[...], preferred_element_type=F32))
    h = h + gate * pe
    o_ref[...] = _rmsnorm(h, gfin_ref[...]) if final else h


def _out_projection(x2d, m_out, d_out, z, p2d, w_out, w_ple, w_gate, g_ple, g_final, final):
    rows, d_model = x2d.shape
    tm = ROW_TILE
    row_spec = lambda a: pl.BlockSpec((tm, a.shape[1]), lambda i: (i, 0))
    full_spec = lambda a: pl.BlockSpec(a.shape, lambda i: (0, 0))
    return pl.pallas_call(
        functools.partial(_outproj_kernel, final=final),
        grid=(rows // tm,),
        in_specs=[row_spec(x2d), row_spec(m_out), row_spec(d_out), row_spec(z), row_spec(p2d),
                  full_spec(w_out), full_spec(w_ple), full_spec(w_gate),
                  full_spec(g_ple), full_spec(g_final)],
        out_specs=pl.BlockSpec((tm, d_model), lambda i: (i, 0)),
        out_shape=jax.ShapeDtypeStruct((rows, d_model), F32),
        compiler_params=pltpu.CompilerParams(
            dimension_semantics=("arbitrary",), vmem_limit_bytes=VMEM_LIMIT_BYTES),
        name="out_projection",
    )(x2d, m_out, d_out, z, p2d, w_out, w_ple, w_gate, g_ple, g_final)


def kernel(x, p, w_in, w_out, g_mix, diff_lq1, diff_lk1, diff_lq2, diff_lk2, g_subln,
           w_ple, w_ple_gate, g_ple, g_final):
    batch, seq, d_model = x.shape
    depth = w_in.shape[0]
    rows = batch * seq
    assert w_in.shape[2] == QKV_W + MOBA_W + DIFF_W
    assert seq % ROW_TILE == 0 and ROW_TILE % MOBA_BLOCK == 0 and Q_TILE == MOBA_BLOCK
    rope_tab = _rope_tables(seq)
    row = lambda a: a.reshape(1, -1)
    h = x.reshape(rows, d_model)
    for i in range(depth):
        lambda_init = 0.8 - 0.6 * float(np.exp(-0.3 * i))
        qkv, z, kmean = _in_projection(h, row(g_mix[i]), w_in[i].astype(BF16), rope_tab, seq)
        qkv = qkv.reshape(batch, seq, QKV_W)
        kmean = kmean.reshape(batch, seq // MOBA_BLOCK, SLAB)
        sec = SLAB // LANES
        m_out = _attention(qkv, (kmean,), moba=True, lambda_init=lambda_init,
                           q_col=0, k_col=sec, v_col=2 * sec, groups=MOBA_W // LANES)
        d_out = _attention(qkv, (row(diff_lq1[i]), row(diff_lk1[i]), row(diff_lq2[i]),
                                 row(diff_lk2[i]), row(g_subln[i])),
                           moba=False, lambda_init=lambda_init,
                           q_col=3 * sec, k_col=4 * sec, v_col=5 * sec, groups=DIFF_W // LANES)
        h = _out_projection(h, m_out.reshape(rows, MOBA_W), d_out.reshape(rows, DIFF_W), z,
                            p[i].reshape(rows, -1), w_out[i].astype(BF16),
                            w_ple[i].astype(BF16), w_ple_gate[i].astype(BF16),
                            row(g_ple[i]), row(g_final), final=(i == depth - 1))
    return h.reshape(batch, seq, d_model)
```

```python
import functools

import numpy as np
import jax
import jax.numpy as jnp
from jax import lax
from jax.experimental import pallas as pl
from jax.experimental.pallas import tpu as pltpu

HEAD_DIM = 64
MOBA_HEADS = 8
MOBA_BLOCK = 256
MOBA_TOPK = 3
DIFF_HEADS = 4
ROPE_THETA = 500000.0
ROPE_DIM = HEAD_DIM // 4
EPS = 1e-6

LANES = 128
MOBA_W = MOBA_HEADS * HEAD_DIM
DIFF_W = DIFF_HEADS * 2 * HEAD_DIM
SLAB = 512
QKV_W = 6 * SLAB
ROW_TILE = 512
Q_TILE = MOBA_BLOCK
VMEM_LIMIT_BYTES = 48 * 1024 * 1024

F32 = jnp.float32
BF16 = jnp.bfloat16
NEG_INF = float("-inf")
MASK_BIAS = -(2.0 ** 100)


def _rope_tables(seq):
    inv = ROPE_THETA ** (-jnp.arange(0, ROPE_DIM, 2, dtype=F32) / ROPE_DIM)
    ang = jnp.arange(seq, dtype=F32)[:, None] * inv[None, :]
    cos, sin = jnp.cos(ang), jnp.sin(ang)
    half = ROPE_DIM // 2
    ones = jnp.ones((seq, HEAD_DIM - ROPE_DIM), F32)
    zeros = lambda n: jnp.zeros((seq, n), F32)
    c = jnp.concatenate([cos, cos, ones], axis=-1)
    s1 = jnp.concatenate([-sin, zeros(HEAD_DIM - half)], axis=-1)
    s2 = jnp.concatenate([zeros(half), sin, zeros(HEAD_DIM - ROPE_DIM)], axis=-1)
    k_tabs = [jnp.concatenate([t, t], axis=-1) for t in (c, s1, s2)]
    scale = HEAD_DIM ** -0.5 * float(np.log2(np.e))
    q_tabs = [t * scale for t in k_tabs]
    return jnp.concatenate(q_tabs + k_tabs, axis=-1)


def _rmsnorm(x, g):
    return x * lax.rsqrt(jnp.mean(x * x, axis=-1, keepdims=True) + EPS) * g


def _inproj_kernel(x_ref, g_ref, w_ref, rope_ref, qkv_ref, gz_ref, kmean_ref):
    tm = x_ref.shape[0]
    u = _rmsnorm(x_ref[...], g_ref[...]).astype(BF16)

    def rope(slab, tab0):
        outs = []
        for c in range(SLAB // LANES):
            t = slab[:, c * LANES:(c + 1) * LANES]
            cos = rope_ref[:, (tab0 + 0) * LANES:(tab0 + 1) * LANES]
            s1 = rope_ref[:, (tab0 + 1) * LANES:(tab0 + 2) * LANES]
            s2 = rope_ref[:, (tab0 + 2) * LANES:(tab0 + 3) * LANES]
            half = ROPE_DIM // 2
            outs.append(t * cos + pltpu.roll(t, LANES - half, 1) * s1 + pltpu.roll(t, half, 1) * s2)
        return jnp.concatenate(outs, axis=-1)

    for sec in range(6):
        slab = jnp.dot(u, w_ref[:, sec * SLAB:(sec + 1) * SLAB], preferred_element_type=F32)
        if sec in (0, 3):
            slab = rope(slab, 0)
        elif sec in (1, 4):
            slab = rope(slab, 3)
        if sec == 1:
            for r in range(tm // MOBA_BLOCK):
                kmean_ref[r] = jnp.mean(slab[r * MOBA_BLOCK:(r + 1) * MOBA_BLOCK], axis=0, keepdims=True)
        qkv_ref[:, sec * SLAB:(sec + 1) * SLAB] = slab.astype(BF16)
    z = jnp.dot(u, w_ref[:, QKV_W:], preferred_element_type=F32)
    gz_ref[...] = z * jax.nn.sigmoid(z)


def _in_projection(x2d, g_mix, w_in_bf16, rope_tab, seq):
    rows, d_model = x2d.shape
    proj_w = w_in_bf16.shape[1]
    tm = ROW_TILE
    seq_tiles = seq // tm
    return pl.pallas_call(
        _inproj_kernel,
        grid=(rows // tm,),
        in_specs=[
            pl.BlockSpec((tm, d_model), lambda i: (i, 0)),
            pl.BlockSpec((1, d_model), lambda i: (0, 0)),
            pl.BlockSpec((d_model, proj_w), lambda i: (0, 0)),
            pl.BlockSpec((tm, 6 * LANES), lambda i: (i % seq_tiles, 0)),
        ],
        out_specs=[
            pl.BlockSpec((tm, QKV_W), lambda i: (i, 0)),
            pl.BlockSpec((tm, proj_w - QKV_W), lambda i: (i, 0)),
            pl.BlockSpec((tm // MOBA_BLOCK, 1, SLAB), lambda i: (i, 0, 0)),
        ],
        out_shape=[
            jax.ShapeDtypeStruct((rows, QKV_W), BF16),
            jax.ShapeDtypeStruct((rows, proj_w - QKV_W), F32),
            jax.ShapeDtypeStruct((rows // MOBA_BLOCK, 1, SLAB), F32),
        ],
        compiler_params=pltpu.CompilerParams(
            dimension_semantics=("arbitrary",), vmem_limit_bytes=VMEM_LIMIT_BYTES),
        name="in_projection",
    )(x2d, g_mix, w_in_bf16, rope_tab)


def _attn_kernel(*refs, moba, lambda_init):
    if moba:
        q_ref, k_ref, v_ref, gz_ref, kmean_ref, o_ref = refs
    else:
        q_ref, k_ref, v_ref, gz_ref, lq1_ref, lk1_ref, lq2_ref, lk2_ref, gsub_ref, o_ref = refs
    seq = q_ref.shape[1]
    tq = Q_TILE
    nt = (((1,), (1,)), ((), ()))

    first_q = lax.broadcasted_iota(jnp.int32, (tq, LANES), 1) < HEAD_DIM
    own_lanes = (first_q, jnp.logical_not(first_q))
    causal = (lax.broadcasted_iota(jnp.int32, (tq, tq), 1)
              <= lax.broadcasted_iota(jnp.int32, (tq, tq), 0))
    k2 = k_ref[0]
    v2 = v_ref[0]
    v_ones = jnp.concatenate([v2, jnp.ones_like(v2)], axis=1)

    if moba:
        lane_k = lax.broadcasted_iota(jnp.int32, (seq, LANES), 1)
        key_blk = lax.broadcasted_iota(jnp.int32, (seq, LANES), 0) // MOBA_BLOCK
        first_k = lane_k < HEAD_DIM
        hot_hi = jnp.where(lane_k - HEAD_DIM == key_blk, 1.0, 0.0).astype(BF16)
        hot_lo = jnp.where(lane_k == key_blk, 1.0, 0.0).astype(BF16)
        ks = (jnp.where(first_k, k2, hot_hi), jnp.where(first_k, hot_lo, k2))
        km = kmean_ref[0]
        km_hi = km.astype(BF16)
        km_lo = (km - km_hi.astype(F32)).astype(BF16)
        nb = km.shape[0]
        blk = lax.broadcasted_iota(jnp.int32, (nb, tq), 0)
    else:
        ks = (k2, k2)
        lam = (jnp.exp(jnp.sum(lq1_ref[...] * lk1_ref[...], axis=1, keepdims=True))
               - jnp.exp(jnp.sum(lq2_ref[...] * lk2_ref[...], axis=1, keepdims=True))
               + lambda_init)

    def block_mask(q2, j):
        rows = []
        for h in (1, 0):
            qh = jnp.where(own_lanes[h], q2, jnp.zeros_like(q2))
            gate = (lax.dot_general(km_hi, qh, nt, preferred_element_type=F32)
                    + lax.dot_general(km_lo, qh, nt, preferred_element_type=F32))
            past = blk < j
            gate = jnp.where(past, gate, NEG_INF)
            beaten = jnp.zeros((nb, tq), jnp.int32)
            for m in range(j):
                gm = gate[m:m + 1, :]
                beats = (gm > gate) | ((gm == gate) & (blk > m))
                beaten = beaten + beats.astype(jnp.int32)
            dropped = past & (beaten >= MOBA_TOPK)
            rows.append(jnp.where(dropped, MASK_BIAS, 0.0).astype(F32))
            rows.append(jnp.zeros((HEAD_DIM - nb, tq), F32))
        return jnp.concatenate(rows, axis=0).T.astype(BF16)

    def query_side(j):
        q2 = q_ref[0, j * tq:(j + 1) * tq, :]
        other = block_mask(q2, j) if (moba and j > MOBA_TOPK) else jnp.zeros_like(q2)
        return [jnp.where(own_lanes[h], q2, other) for h in range(2)]

    def scores(qa, j, h):
        return lax.dot_general(qa, ks[h][:(j + 1) * tq], nt, preferred_element_type=F32)

    def weighted_values(s, j):
        length = (j + 1) * tq
        s_own = jnp.where(causal, s[:, length - tq:], NEG_INF)
        m = jnp.max(s_own, axis=1, keepdims=True)
        if j > 0:
            s_past = s[:, :length - tq]
            m = jnp.maximum(m, jnp.max(s_past, axis=1, keepdims=True))
        p = jnp.exp2(s_own - m)
        if j > 0:
            p = jnp.concatenate([jnp.exp2(s_past - m), p], axis=1)
        acc = jnp.dot(p.astype(BF16), v_ones[:length], preferred_element_type=F32)
        return acc[:, :LANES] / acc[:, LANES:]

    units = [(j, h) for j in reversed(range(seq // tq)) for h in range(2)]
    qa = query_side(units[0][0])
    s_next = scores(qa[0], *units[0])
    outs = []
    for idx, (j, h) in enumerate(units):
        s = s_next
        if idx + 1 < len(units):
            jn, hn = units[idx + 1]
            if hn == 0:
                qa = query_side(jn)
            s_next = scores(qa[hn], jn, hn)
        outs.append(weighted_values(s, j))
        if h == 1:
            if moba:
                out = jnp.where(first_q, outs[0], outs[1])
            else:
                out = _rmsnorm(outs[0] - lam * outs[1], gsub_ref[...]) * (1.0 - lambda_init)
            rows = slice(j * tq, (j + 1) * tq)
            o_ref[0, rows, :] = (out * gz_ref[0, rows, :]).astype(BF16)
            outs = []


def _attention(qkv, gz, extra, *, moba, lambda_init, q_col, k_col, v_col, gz_col, groups):
    batch, seq, _ = qkv.shape
    col_spec = lambda col: pl.BlockSpec((1, seq, LANES), lambda b, g: (b, 0, col + g))
    in_specs = [col_spec(q_col), col_spec(k_col), col_spec(v_col), col_spec(gz_col)]
    if moba:
        (kmean,) = extra
        in_specs.append(pl.BlockSpec((1, kmean.shape[1], LANES), lambda b, g: (b, 0, g)))
    else:
        for a in extra:
            in_specs.append(pl.BlockSpec(a.shape, lambda b, g: (0, 0)))
    return pl.pallas_call(
        functools.partial(_attn_kernel, moba=moba, lambda_init=lambda_init),
        grid=(batch, groups),
        in_specs=in_specs,
        out_specs=pl.BlockSpec((1, seq, LANES), lambda b, g: (b, 0, g)),
        out_shape=jax.ShapeDtypeStruct((batch, seq, groups * LANES), BF16),
        compiler_params=pltpu.CompilerParams(
            dimension_semantics=("arbitrary", "arbitrary"),
            vmem_limit_bytes=VMEM_LIMIT_BYTES),
        name="moba_attention" if moba else "diff_attention",
    )(qkv, qkv, qkv, gz, *extra)


def _outproj_kernel(x_ref, ym_ref, yd_ref, p_ref, wout_ref, wple_ref, wgate_ref,
                    gple_ref, gfin_ref, o_ref, *, final):
    mw = ym_ref.shape[1]
    h = (x_ref[...]
         + jnp.dot(ym_ref[...], wout_ref[:mw, :], preferred_element_type=F32)
         + jnp.dot(yd_ref[...], wout_ref[mw:, :], preferred_element_type=F32))
    pe = jnp.dot(p_ref[...].astype(BF16), wple_ref[...], preferred_element_type=F32)
    hn = _rmsnorm(h, gple_ref[...]).astype(BF16)
    gate = jax.nn.sigmoid(jnp.dot(hn, wgate_ref[...], preferred_element_type=F32))
    h = h + gate * pe
    o_ref[...] = _rmsnorm(h, gfin_ref[...]) if final else h


def _out_projection(x2d, y_moba, y_diff, p2d, w_out, w_ple, w_gate, g_ple, g_final, final):
    rows, d_model = x2d.shape
    tm = ROW_TILE
    row_spec = lambda a: pl.BlockSpec((tm, a.shape[1]), lambda i: (i, 0))
    full_spec = lambda a: pl.BlockSpec(a.shape, lambda i: (0, 0))
    return pl.pallas_call(
        functools.partial(_outproj_kernel, final=final),
        grid=(rows // tm,),
        in_specs=[row_spec(x2d), row_spec(y_moba), row_spec(y_diff), row_spec(p2d),
                  full_spec(w_out), full_spec(w_ple), full_spec(w_gate),
                  full_spec(g_ple), full_spec(g_final)],
        out_specs=pl.BlockSpec((tm, d_model), lambda i: (i, 0)),
        out_shape=jax.ShapeDtypeStruct((rows, d_model), F32),
        compiler_params=pltpu.CompilerParams(
            dimension_semantics=("arbitrary",), vmem_limit_bytes=VMEM_LIMIT_BYTES),
        name="out_projection",
    )(x2d, y_moba, y_diff, p2d, w_out, w_ple, w_gate, g_ple, g_final)


def kernel(x, p, w_in, w_out, g_mix, diff_lq1, diff_lk1, diff_lq2, diff_lk2, g_subln,
           w_ple, w_ple_gate, g_ple, g_final):
    batch, seq, d_model = x.shape
    depth = w_in.shape[0]
    rows = batch * seq
    assert w_in.shape[2] == QKV_W + MOBA_W + DIFF_W
    assert seq % ROW_TILE == 0 and ROW_TILE % MOBA_BLOCK == 0 and Q_TILE == MOBA_BLOCK
    rope_tab = _rope_tables(seq)
    row = lambda a: a.reshape(1, -1)
    h = x.reshape(rows, d_model)
    for i in range(depth):
        lambda_init = 0.8 - 0.6 * float(np.exp(-0.3 * i))
        qkv, gz, kmean = _in_projection(h, row(g_mix[i]), w_in[i].astype(BF16), rope_tab, seq)
        qkv = qkv.reshape(batch, seq, QKV_W)
        gz = gz.reshape(batch, seq, MOBA_W + DIFF_W)
        kmean = kmean.reshape(batch, seq // MOBA_BLOCK, SLAB)
        sec = SLAB // LANES
        y_moba = _attention(qkv, gz, (kmean,), moba=True, lambda_init=lambda_init,
                            q_col=0, k_col=sec, v_col=2 * sec, gz_col=0, groups=MOBA_W // LANES)
        y_diff = _attention(qkv, gz, (row(diff_lq1[i]), row(diff_lk1[i]), row(diff_lq2[i]),
                                      row(diff_lk2[i]), row(g_subln[i])),
                            moba=False, lambda_init=lambda_init,
                            q_col=3 * sec, k_col=4 * sec, v_col=5 * sec, gz_col=MOBA_W // LANES,
                            groups=DIFF_W // LANES)
        h = _out_projection(h, y_moba.reshape(rows, MOBA_W), y_diff.reshape(rows, DIFF_W),
                            p[i].reshape(rows, -1), w_out[i].astype(BF16),
                            w_ple[i].astype(BF16), w_ple_gate[i].astype(BF16),
                            row(g_ple[i]), row(g_final), final=(i == depth - 1))
    return h.reshape(batch, seq, d_model)
```

```python
import functools

import numpy as np
import jax
import jax.numpy as jnp
from jax import lax
from jax.experimental import pallas as pl
from jax.experimental.pallas import tpu as pltpu

HEAD_DIM = 64
MOBA_HEADS = 8
MOBA_BLOCK = 256
MOBA_TOPK = 3
DIFF_HEADS = 4
ROPE_THETA = 500000.0
ROPE_DIM = HEAD_DIM // 4
EPS = 1e-6

LANES = 128
MOBA_W = MOBA_HEADS * HEAD_DIM
DIFF_W = DIFF_HEADS * 2 * HEAD_DIM
SLAB = 512
QKV_W = 6 * SLAB
ROW_TILE = 512
Q_TILE = MOBA_BLOCK
MOBA_QK_LOOKAHEAD = 3
DIFF_QK_LOOKAHEAD = 5
VMEM_LIMIT_BYTES = 48 * 1024 * 1024

F32 = jnp.float32
BF16 = jnp.bfloat16
NEG_INF = float("-inf")
MASK_BIAS = -(2.0 ** 100)
ONES_ROWS = 16


def _rope_tables(seq):
    inv = ROPE_THETA ** (-jnp.arange(0, ROPE_DIM, 2, dtype=F32) / ROPE_DIM)
    ang = jnp.arange(seq, dtype=F32)[:, None] * inv[None, :]
    cos, sin = jnp.cos(ang), jnp.sin(ang)
    half = ROPE_DIM // 2
    ones = jnp.ones((seq, HEAD_DIM - ROPE_DIM), F32)
    zeros = lambda n: jnp.zeros((seq, n), F32)
    c = jnp.concatenate([cos, cos, ones], axis=-1)
    s1 = jnp.concatenate([-sin, zeros(HEAD_DIM - half)], axis=-1)
    s2 = jnp.concatenate([zeros(half), sin, zeros(HEAD_DIM - ROPE_DIM)], axis=-1)
    k_tabs = [jnp.concatenate([t, t], axis=-1) for t in (c, s1, s2)]
    scale = HEAD_DIM ** -0.5 * float(np.log2(np.e))
    q_tabs = [t * scale for t in k_tabs]
    return jnp.concatenate(q_tabs + k_tabs, axis=-1)


def _rmsnorm(x, g):
    return x * lax.rsqrt(jnp.mean(x * x, axis=-1, keepdims=True) + EPS) * g


def _inproj_kernel(x_ref, g_ref, w_ref, rope_ref, qt_ref, k_ref, vt_ref, gz_ref, kmean_ref):
    tm = x_ref.shape[0]
    u = _rmsnorm(x_ref[...], g_ref[...]).astype(BF16)

    def rope(slab, tab0):
        outs = []
        for c in range(SLAB // LANES):
            t = slab[:, c * LANES:(c + 1) * LANES]
            cos = rope_ref[:, (tab0 + 0) * LANES:(tab0 + 1) * LANES]
            s1 = rope_ref[:, (tab0 + 1) * LANES:(tab0 + 2) * LANES]
            s2 = rope_ref[:, (tab0 + 2) * LANES:(tab0 + 3) * LANES]
            half = ROPE_DIM // 2
            outs.append(t * cos + pltpu.roll(t, LANES - half, 1) * s1 + pltpu.roll(t, half, 1) * s2)
        return jnp.concatenate(outs, axis=-1)

    for sec in range(6):
        slab = jnp.dot(u, w_ref[:, sec * SLAB:(sec + 1) * SLAB], preferred_element_type=F32)
        head_cols = slice((sec // 3) * SLAB, (sec // 3 + 1) * SLAB)
        if sec % 3 == 0:
            qt_ref[0, head_cols, :] = rope(slab, 0).T.astype(BF16)
        elif sec % 3 == 1:
            slab = rope(slab, 3)
            if sec == 1:
                for r in range(tm // MOBA_BLOCK):
                    kmean_ref[r] = jnp.mean(slab[r * MOBA_BLOCK:(r + 1) * MOBA_BLOCK], axis=0, keepdims=True)
            k_ref[:, head_cols] = slab.astype(BF16)
        else:
            vt_ref[0, head_cols, :] = slab.T.astype(BF16)
    z = jnp.dot(u, w_ref[:, QKV_W:], preferred_element_type=F32)
    gz_ref[...] = z * jax.nn.sigmoid(z)


def _in_projection(x2d, g_mix, w_in_bf16, rope_tab, batch, seq):
    rows, d_model = x2d.shape
    proj_w = w_in_bf16.shape[1]
    tm = ROW_TILE
    seq_tiles = seq // tm
    heads_w = 2 * SLAB
    t_spec = pl.BlockSpec((1, heads_w, tm), lambda i: (i // seq_tiles, 0, i % seq_tiles))
    return pl.pallas_call(
        _inproj_kernel,
        grid=(rows // tm,),
        in_specs=[
            pl.BlockSpec((tm, d_model), lambda i: (i, 0)),
            pl.BlockSpec((1, d_model), lambda i: (0, 0)),
            pl.BlockSpec((d_model, proj_w), lambda i: (0, 0)),
            pl.BlockSpec((tm, 6 * LANES), lambda i: (i % seq_tiles, 0)),
        ],
        out_specs=[
            t_spec,
            pl.BlockSpec((tm, heads_w), lambda i: (i, 0)),
            t_spec,
            pl.BlockSpec((tm, proj_w - QKV_W), lambda i: (i, 0)),
            pl.BlockSpec((tm // MOBA_BLOCK, 1, SLAB), lambda i: (i, 0, 0)),
        ],
        out_shape=[
            jax.ShapeDtypeStruct((batch, heads_w, seq), BF16),
            jax.ShapeDtypeStruct((rows, heads_w), BF16),
            jax.ShapeDtypeStruct((batch, heads_w, seq), BF16),
            jax.ShapeDtypeStruct((rows, proj_w - QKV_W), F32),
            jax.ShapeDtypeStruct((rows // MOBA_BLOCK, 1, SLAB), F32),
        ],
        compiler_params=pltpu.CompilerParams(
            dimension_semantics=("arbitrary",), vmem_limit_bytes=VMEM_LIMIT_BYTES),
        name="in_projection",
    )(x2d, g_mix, w_in_bf16, rope_tab)


def _attn_kernel(*refs, moba, lambda_init):
    if moba:
        qt_ref, k_ref, vt_ref, gz_ref, kmean_ref, o_ref = refs
    else:
        qt_ref, k_ref, vt_ref, gz_ref, lq1_ref, lk1_ref, lq2_ref, lk2_ref, gsub_ref, o_ref = refs
    seq = k_ref.shape[1]
    tq = Q_TILE

    first_rows = lax.broadcasted_iota(jnp.int32, (LANES, tq), 0) < HEAD_DIM
    own_rows = (first_rows, jnp.logical_not(first_rows))
    causal = (lax.broadcasted_iota(jnp.int32, (tq, tq), 0)
              <= lax.broadcasted_iota(jnp.int32, (tq, tq), 1))
    k2 = k_ref[0]
    vt_ones = jnp.concatenate([vt_ref[0], jnp.ones((ONES_ROWS, seq), BF16)], axis=0)

    if moba:
        lane_k = lax.broadcasted_iota(jnp.int32, (seq, LANES), 1)
        key_blk = lax.broadcasted_iota(jnp.int32, (seq, LANES), 0) // MOBA_BLOCK
        first_k = lane_k < HEAD_DIM
        hot_hi = jnp.where(lane_k - HEAD_DIM == key_blk, 1.0, 0.0).astype(BF16)
        hot_lo = jnp.where(lane_k == key_blk, 1.0, 0.0).astype(BF16)
        ks = (jnp.where(first_k, k2, hot_hi), jnp.where(first_k, hot_lo, k2))
        km = kmean_ref[0]
        km_hi = km.astype(BF16)
        km_lo = (km - km_hi.astype(F32)).astype(BF16)
        nb = km.shape[0]
        blk = lax.broadcasted_iota(jnp.int32, (nb, tq), 0)
    else:
        ks = (k2, k2)
        lam = (jnp.exp(jnp.sum(lq1_ref[...] * lk1_ref[...], axis=1, keepdims=True))
               - jnp.exp(jnp.sum(lq2_ref[...] * lk2_ref[...], axis=1, keepdims=True))
               + lambda_init)

    def block_mask(qt, j):
        rows = []
        for h in (1, 0):
            qh = jnp.where(own_rows[h], qt, jnp.zeros_like(qt))
            gate = (jnp.dot(km_hi, qh, preferred_element_type=F32)
                    + jnp.dot(km_lo, qh, preferred_element_type=F32))
            past = blk < j
            gate = jnp.where(past, gate, NEG_INF)
            beaten = jnp.zeros((nb, tq), jnp.int32)
            for m in range(j):
                gm = gate[m:m + 1, :]
                beats = (gm > gate) | ((gm == gate) & (blk > m))
                beaten = beaten + beats.astype(jnp.int32)
            dropped = past & (beaten >= MOBA_TOPK)
            rows.append(jnp.where(dropped, MASK_BIAS, 0.0).astype(F32))
            rows.append(jnp.zeros((HEAD_DIM - nb, tq), F32))
        return jnp.concatenate(rows, axis=0).astype(BF16)

    def query_side(j):
        qt = qt_ref[0, :, j * tq:(j + 1) * tq]
        other = block_mask(qt, j) if (moba and j > MOBA_TOPK) else jnp.zeros_like(qt)
        return [jnp.where(own_rows[h], qt, other) for h in range(2)]

    def scores(qa, j, h):
        return jnp.dot(ks[h][:(j + 1) * tq], qa, preferred_element_type=F32)

    def weights(s, j):
        length = (j + 1) * tq
        s_own = jnp.where(causal, s[length - tq:], NEG_INF)
        m = jnp.max(s_own, axis=0, keepdims=True)
        if j > 0:
            s_past = s[:length - tq]
            m = jnp.maximum(m, jnp.max(s_past, axis=0, keepdims=True))
        p = jnp.exp2(s_own - m)
        if j > 0:
            p = jnp.concatenate([jnp.exp2(s_past - m), p], axis=0)
        return p.astype(BF16)

    def finish(p_pair, j):
        length = (j + 1) * tq
        p_both = jnp.concatenate(p_pair, axis=1)
        acc = jnp.dot(vt_ones[:, :length], p_both, preferred_element_type=F32)
        o = acc[:LANES] / acc[LANES:LANES + 1]
        o0, o1 = o[:, :tq], o[:, tq:]
        if moba:
            out = jnp.where(first_rows, o0, o1).T
        else:
            out = _rmsnorm((o0 - lam * o1).T, gsub_ref[...]) * (1.0 - lambda_init)
        rows = slice(j * tq, (j + 1) * tq)
        o_ref[0, rows, :] = (out * gz_ref[0, rows, :]).astype(BF16)

    lookahead = MOBA_QK_LOOKAHEAD if moba else DIFF_QK_LOOKAHEAD
    units = [(j, h) for j in reversed(range(seq // tq)) for h in range(2)]
    pending = []

    def issue(idx):
        jn, hn = units[idx]
        if hn == 0:
            issue.qa = query_side(jn)
        pending.append(scores(issue.qa[hn], jn, hn))

    for idx in range(min(lookahead, len(units))):
        issue(idx)
    p_pair = []
    for idx, (j, h) in enumerate(units):
        if idx + lookahead < len(units):
            issue(idx + lookahead)
        p_pair.append(weights(pending.pop(0), j))
        if h == 1:
            finish(p_pair, j)
            p_pair = []


def _attention(qt, k, vt, gz, extra, *, moba, lambda_init, col, groups):
    batch, seq, _ = k.shape
    t_spec = pl.BlockSpec((1, LANES, seq), lambda b, g: (b, col + g, 0))
    row_spec = pl.BlockSpec((1, seq, LANES), lambda b, g: (b, 0, col + g))
    in_specs = [t_spec, row_spec, t_spec, row_spec]
    if moba:
        (kmean,) = extra
        in_specs.append(pl.BlockSpec((1, kmean.shape[1], LANES), lambda b, g: (b, 0, g)))
    else:
        for a in extra:
            in_specs.append(pl.BlockSpec(a.shape, lambda b, g: (0, 0)))
    return pl.pallas_call(
        functools.partial(_attn_kernel, moba=moba, lambda_init=lambda_init),
        grid=(batch, groups),
        in_specs=in_specs,
        out_specs=pl.BlockSpec((1, seq, LANES), lambda b, g: (b, 0, g)),
        out_shape=jax.ShapeDtypeStruct((batch, seq, groups * LANES), BF16),
        compiler_params=pltpu.CompilerParams(
            dimension_semantics=("arbitrary", "arbitrary"),
            vmem_limit_bytes=VMEM_LIMIT_BYTES),
        name="moba_attention" if moba else "diff_attention",
    )(qt, k, vt, gz, *extra)


def _outproj_kernel(x_ref, ym_ref, yd_ref, p_ref, wout_ref, wple_ref, wgate_ref,
                    gple_ref, gfin_ref, o_ref, *, final):
    mw = ym_ref.shape[1]
    h = (x_ref[...]
         + jnp.dot(ym_ref[...], wout_ref[:mw, :], preferred_element_type=F32)
         + jnp.dot(yd_ref[...], wout_ref[mw:, :], preferred_element_type=F32))
    pe = jnp.dot(p_ref[...].astype(BF16), wple_ref[...], preferred_element_type=F32)
    hn = _rmsnorm(h, gple_ref[...]).astype(BF16)
    gate = jax.nn.sigmoid(jnp.dot(hn, wgate_ref[...], preferred_element_type=F32))
    h = h + gate * pe
    o_ref[...] = _rmsnorm(h, gfin_ref[...]) if final else h


def _out_projection(x2d, y_moba, y_diff, p2d, w_out, w_ple, w_gate, g_ple, g_final, final):
    rows, d_model = x2d.shape
    tm = ROW_TILE
    row_spec = lambda a: pl.BlockSpec((tm, a.shape[1]), lambda i: (i, 0))
    full_spec = lambda a: pl.BlockSpec(a.shape, lambda i: (0, 0))
    return pl.pallas_call(
        functools.partial(_outproj_kernel, final=final),
        grid=(rows // tm,),
        in_specs=[row_spec(x2d), row_spec(y_moba), row_spec(y_diff), row_spec(p2d),
                  full_spec(w_out), full_spec(w_ple), full_spec(w_gate),
                  full_spec(g_ple), full_spec(g_final)],
        out_specs=pl.BlockSpec((tm, d_model), lambda i: (i, 0)),
        out_shape=jax.ShapeDtypeStruct((rows, d_model), F32),
        compiler_params=pltpu.CompilerParams(
            dimension_semantics=("arbitrary",), vmem_limit_bytes=VMEM_LIMIT_BYTES),
        name="out_projection",
    )(x2d, y_moba, y_diff, p2d, w_out, w_ple, w_gate, g_ple, g_final)


def kernel(x, p, w_in, w_out, g_mix, diff_lq1, diff_lk1, diff_lq2, diff_lk2, g_subln,
           w_ple, w_ple_gate, g_ple, g_final):
    batch, seq, d_model = x.shape
    depth = w_in.shape[0]
    rows = batch * seq
    assert w_in.shape[2] == QKV_W + MOBA_W + DIFF_W
    assert seq % ROW_TILE == 0 and ROW_TILE % MOBA_BLOCK == 0 and Q_TILE == MOBA_BLOCK
    rope_tab = _rope_tables(seq)
    row = lambda a: a.reshape(1, -1)
    h = x.reshape(rows, d_model)
    for i in range(depth):
        lambda_init = 0.8 - 0.6 * float(np.exp(-0.3 * i))
        qt, k, vt, gz, kmean = _in_projection(h, row(g_mix[i]), w_in[i].astype(BF16), rope_tab,
                                              batch, seq)
        k = k.reshape(batch, seq, 2 * SLAB)
        gz = gz.reshape(batch, seq, MOBA_W + DIFF_W)
        kmean = kmean.reshape(batch, seq // MOBA_BLOCK, SLAB)
        moba_groups = MOBA_W // LANES
        y_moba = _attention(qt, k, vt, gz, (kmean,), moba=True, lambda_init=lambda_init,
                            col=0, groups=moba_groups)
        y_diff = _attention(qt, k, vt, gz, (row(diff_lq1[i]), row(diff_lk1[i]), row(diff_lq2[i]),
                                            row(diff_lk2[i]), row(g_subln[i])),
                            moba=False, lambda_init=lambda_init,
                            col=moba_groups, groups=DIFF_W // LANES)
        h = _out_projection(h, y_moba.reshape(rows, MOBA_W), y_diff.reshape(rows, DIFF_W),
                            p[i].reshape(rows, -1), w_out[i].astype(BF16),
                            w_ple[i].astype(BF16), w_ple_gate[i].astype(BF16),
                            row(g_ple[i]), row(g_final), final=(i == depth - 1))
    return h.reshape(batch, seq, d_model)
```

```python
import functools

import numpy as np
import jax
import jax.numpy as jnp
from jax import lax
from jax.experimental import pallas as pl
from jax.experimental.pallas import tpu as pltpu

HEAD_DIM = 64
MOBA_HEADS = 8
MOBA_BLOCK = 256
MOBA_TOPK = 3
DIFF_HEADS = 4
ROPE_THETA = 500000.0
ROPE_DIM = HEAD_DIM // 4
EPS = 1e-6

LANES = 128
MOBA_W = MOBA_HEADS * HEAD_DIM
DIFF_W = DIFF_HEADS * 2 * HEAD_DIM
SLAB = 512
QKV_W = 6 * SLAB
ROW_TILE = 512
OUT_ROW_TILE = 1024
OUT_SUBTILES = 4
Q_TILE = MOBA_BLOCK
MOBA_QK_LOOKAHEAD = 3
DIFF_QK_LOOKAHEAD = 5
VMEM_LIMIT_BYTES = 48 * 1024 * 1024

F32 = jnp.float32
BF16 = jnp.bfloat16
NEG_INF = float("-inf")
MASK_BIAS = -(2.0 ** 100)
ONES_ROWS = 16


def _rope_tables(seq):
    inv = ROPE_THETA ** (-jnp.arange(0, ROPE_DIM, 2, dtype=F32) / ROPE_DIM)
    ang = jnp.arange(seq, dtype=F32)[:, None] * inv[None, :]
    cos, sin = jnp.cos(ang), jnp.sin(ang)
    half = ROPE_DIM // 2
    ones = jnp.ones((seq, HEAD_DIM - ROPE_DIM), F32)
    zeros = lambda n: jnp.zeros((seq, n), F32)
    c = jnp.concatenate([cos, cos, ones], axis=-1)
    s1 = jnp.concatenate([-sin, zeros(HEAD_DIM - half)], axis=-1)
    s2 = jnp.concatenate([zeros(half), sin, zeros(HEAD_DIM - ROPE_DIM)], axis=-1)
    k_tabs = [jnp.concatenate([t, t], axis=-1) for t in (c, s1, s2)]
    scale = HEAD_DIM ** -0.5 * float(np.log2(np.e))
    q_tabs = [t * scale for t in k_tabs]
    return jnp.concatenate(q_tabs + k_tabs, axis=-1)


def _rmsnorm(x, g):
    return x * lax.rsqrt(jnp.mean(x * x, axis=-1, keepdims=True) + EPS) * g


def _inproj_kernel(x_ref, g_ref, w_ref, rope_ref, qt_ref, k_ref, vt_ref, gz_ref, kmean_ref):
    tm = x_ref.shape[0]

    def rope(slab, rows, tab0):
        outs = []
        for c in range(SLAB // LANES):
            t = slab[:, c * LANES:(c + 1) * LANES]
            cos = rope_ref[rows, (tab0 + 0) * LANES:(tab0 + 1) * LANES]
            s1 = rope_ref[rows, (tab0 + 1) * LANES:(tab0 + 2) * LANES]
            s2 = rope_ref[rows, (tab0 + 2) * LANES:(tab0 + 3) * LANES]
            half = ROPE_DIM // 2
            outs.append(t * cos + pltpu.roll(t, LANES - half, 1) * s1 + pltpu.roll(t, half, 1) * s2)
        return jnp.concatenate(outs, axis=-1)

    for blk in range(tm // MOBA_BLOCK):
        rows = slice(blk * MOBA_BLOCK, (blk + 1) * MOBA_BLOCK)
        u = _rmsnorm(x_ref[rows, :], g_ref[...]).astype(BF16)
        z = jnp.dot(u, w_ref[:, QKV_W:], preferred_element_type=F32)
        gz_ref[rows, :] = z * jax.nn.sigmoid(z)
        for sec in range(6):
            slab = jnp.dot(u, w_ref[:, sec * SLAB:(sec + 1) * SLAB], preferred_element_type=F32)
            head_cols = slice((sec // 3) * SLAB, (sec // 3 + 1) * SLAB)
            if sec % 3 == 0:
                qt_ref[0, head_cols, rows] = rope(slab, rows, 0).T.astype(BF16)
            elif sec % 3 == 1:
                slab = rope(slab, rows, 3)
                if sec == 1:
                    kmean_ref[blk] = jnp.mean(slab, axis=0, keepdims=True)
                k_ref[rows, head_cols] = slab.astype(BF16)
            else:
                vt_ref[0, head_cols, rows] = slab.T.astype(BF16)


def _in_projection(x2d, g_mix, w_in_bf16, rope_tab, batch, seq):
    rows, d_model = x2d.shape
    proj_w = w_in_bf16.shape[1]
    tm = ROW_TILE
    seq_tiles = seq // tm
    heads_w = 2 * SLAB
    t_spec = pl.BlockSpec((1, heads_w, tm), lambda i: (i // seq_tiles, 0, i % seq_tiles))
    return pl.pallas_call(
        _inproj_kernel,
        grid=(rows // tm,),
        in_specs=[
            pl.BlockSpec((tm, d_model), lambda i: (i, 0)),
            pl.BlockSpec((1, d_model), lambda i: (0, 0)),
            pl.BlockSpec((d_model, proj_w), lambda i: (0, 0)),
            pl.BlockSpec((tm, 6 * LANES), lambda i: (i % seq_tiles, 0)),
        ],
        out_specs=[
            t_spec,
            pl.BlockSpec((tm, heads_w), lambda i: (i, 0)),
            t_spec,
            pl.BlockSpec((tm, proj_w - QKV_W), lambda i: (i, 0)),
            pl.BlockSpec((tm // MOBA_BLOCK, 1, SLAB), lambda i: (i, 0, 0)),
        ],
        out_shape=[
            jax.ShapeDtypeStruct((batch, heads_w, seq), BF16),
            jax.ShapeDtypeStruct((rows, heads_w), BF16),
            jax.ShapeDtypeStruct((batch, heads_w, seq), BF16),
            jax.ShapeDtypeStruct((rows, proj_w - QKV_W), F32),
            jax.ShapeDtypeStruct((rows // MOBA_BLOCK, 1, SLAB), F32),
        ],
        compiler_params=pltpu.CompilerParams(
            dimension_semantics=("arbitrary",), vmem_limit_bytes=VMEM_LIMIT_BYTES),
        name="in_projection",
    )(x2d, g_mix, w_in_bf16, rope_tab)


def _attn_kernel(*refs, moba, lambda_init):
    if moba:
        qt_ref, k_ref, vt_ref, gz_ref, kmean_ref, o_ref = refs
    else:
        qt_ref, k_ref, vt_ref, gz_ref, lq1_ref, lk1_ref, lq2_ref, lk2_ref, gsub_ref, o_ref = refs
    seq = k_ref.shape[1]
    tq = Q_TILE

    first_rows = lax.broadcasted_iota(jnp.int32, (LANES, tq), 0) < HEAD_DIM
    own_rows = (first_rows, jnp.logical_not(first_rows))
    causal = (lax.broadcasted_iota(jnp.int32, (tq, tq), 0)
              <= lax.broadcasted_iota(jnp.int32, (tq, tq), 1))
    k2 = k_ref[0]
    vt_ones = jnp.concatenate([vt_ref[0], jnp.ones((ONES_ROWS, seq), BF16)], axis=0)

    if moba:
        lane_k = lax.broadcasted_iota(jnp.int32, (seq, LANES), 1)
        key_blk = lax.broadcasted_iota(jnp.int32, (seq, LANES), 0) // MOBA_BLOCK
        first_k = lane_k < HEAD_DIM
        hot_hi = jnp.where(lane_k - HEAD_DIM == key_blk, 1.0, 0.0).astype(BF16)
        hot_lo = jnp.where(lane_k == key_blk, 1.0, 0.0).astype(BF16)
        ks = (jnp.where(first_k, k2, hot_hi), jnp.where(first_k, hot_lo, k2))
        km = kmean_ref[0]
        km_hi = km.astype(BF16)
        km_lo = (km - km_hi.astype(F32)).astype(BF16)
        nb = km.shape[0]
        blk = lax.broadcasted_iota(jnp.int32, (nb, tq), 0)
    else:
        ks = (k2, k2)
        lam = (jnp.exp(jnp.sum(lq1_ref[...] * lk1_ref[...], axis=1, keepdims=True))
               - jnp.exp(jnp.sum(lq2_ref[...] * lk2_ref[...], axis=1, keepdims=True))
               + lambda_init)

    def block_mask(qt, j):
        rows = []
        for h in (1, 0):
            qh = jnp.where(own_rows[h], qt, jnp.zeros_like(qt))
            gate = (jnp.dot(km_hi, qh, preferred_element_type=F32)
                    + jnp.dot(km_lo, qh, preferred_element_type=F32))
            past = blk < j
            gate = jnp.where(past, gate, NEG_INF)
            beaten = jnp.zeros((nb, tq), jnp.int32)
            for m in range(j):
                gm = gate[m:m + 1, :]
                beats = (gm > gate) | ((gm == gate) & (blk > m))
                beaten = beaten + beats.astype(jnp.int32)
            dropped = past & (beaten >= MOBA_TOPK)
            rows.append(jnp.where(dropped, MASK_BIAS, 0.0).astype(F32))
            rows.append(jnp.zeros((HEAD_DIM - nb, tq), F32))
        return jnp.concatenate(rows, axis=0).astype(BF16)

    def query_side(j):
        qt = qt_ref[0, :, j * tq:(j + 1) * tq]
        other = block_mask(qt, j) if (moba and j > MOBA_TOPK) else jnp.zeros_like(qt)
        return [jnp.where(own_rows[h], qt, other) for h in range(2)]

    def scores(qa, j, h):
        return jnp.dot(ks[h][:(j + 1) * tq], qa, preferred_element_type=F32)

    def weights(s, j):
        length = (j + 1) * tq
        s_own = jnp.where(causal, s[length - tq:], NEG_INF)
        m = jnp.max(s_own, axis=0, keepdims=True)
        if j > 0:
            s_past = s[:length - tq]
            m = jnp.maximum(m, jnp.max(s_past, axis=0, keepdims=True))
        p = jnp.exp2(s_own - m)
        if j > 0:
            p = jnp.concatenate([jnp.exp2(s_past - m), p], axis=0)
        return p.astype(BF16)

    def finish(p_pair, j):
        length = (j + 1) * tq
        p_both = jnp.concatenate(p_pair, axis=1)
        acc = jnp.dot(vt_ones[:, :length], p_both, preferred_element_type=F32)
        o = acc[:LANES] / acc[LANES:LANES + 1]
        o0, o1 = o[:, :tq], o[:, tq:]
        if moba:
            out = jnp.where(first_rows, o0, o1).T
        else:
            out = _rmsnorm((o0 - lam * o1).T, gsub_ref[...]) * (1.0 - lambda_init)
        rows = slice(j * tq, (j + 1) * tq)
        o_ref[0, rows, :] = (out * gz_ref[0, rows, :]).astype(BF16)

    lookahead = MOBA_QK_LOOKAHEAD if moba else DIFF_QK_LOOKAHEAD
    units = [(j, h) for j in reversed(range(seq // tq)) for h in range(2)]
    pending = []

    def issue(idx):
        jn, hn = units[idx]
        if hn == 0:
            issue.qa = query_side(jn)
        pending.append(scores(issue.qa[hn], jn, hn))

    for idx in range(min(lookahead, len(units))):
        issue(idx)
    p_pair = []
    for idx, (j, h) in enumerate(units):
        if idx + lookahead < len(units):
            issue(idx + lookahead)
        p_pair.append(weights(pending.pop(0), j))
        if h == 1:
            finish(p_pair, j)
            p_pair = []


def _attention(qt, k, vt, gz, extra, *, moba, lambda_init, col, groups):
    batch, seq, _ = k.shape
    t_spec = pl.BlockSpec((1, LANES, seq), lambda b, g: (b, col + g, 0))
    row_spec = pl.BlockSpec((1, seq, LANES), lambda b, g: (b, 0, col + g))
    in_specs = [t_spec, row_spec, t_spec, row_spec]
    if moba:
        (kmean,) = extra
        in_specs.append(pl.BlockSpec((1, kmean.shape[1], LANES), lambda b, g: (b, 0, g)))
    else:
        for a in extra:
            in_specs.append(pl.BlockSpec(a.shape, lambda b, g: (0, 0)))
    return pl.pallas_call(
        functools.partial(_attn_kernel, moba=moba, lambda_init=lambda_init),
        grid=(batch, groups),
        in_specs=in_specs,
        out_specs=pl.BlockSpec((1, seq, LANES), lambda b, g: (b, 0, g)),
        out_shape=jax.ShapeDtypeStruct((batch, seq, groups * LANES), BF16),
        compiler_params=pltpu.CompilerParams(
            dimension_semantics=("arbitrary", "arbitrary"),
            vmem_limit_bytes=VMEM_LIMIT_BYTES),
        name="moba_attention" if moba else "diff_attention",
    )(qt, k, vt, gz, *extra)


def _outproj_kernel(x_ref, ym_ref, yd_ref, p_ref, wout_ref, wple_ref, wgate_ref,
                    gple_ref, gfin_ref, o_ref, *, final):
    mw = ym_ref.shape[1]
    tm = x_ref.shape[0]
    subs = [slice(r * tm // OUT_SUBTILES, (r + 1) * tm // OUT_SUBTILES) for r in range(OUT_SUBTILES)]
    hs = [x_ref[r, :]
          + jnp.dot(ym_ref[r, :], wout_ref[:mw, :], preferred_element_type=F32)
          + jnp.dot(yd_ref[r, :], wout_ref[mw:, :], preferred_element_type=F32) for r in subs]
    pes = [jnp.dot(p_ref[r, :].astype(BF16), wple_ref[...], preferred_element_type=F32) for r in subs]
    gates = [jax.nn.sigmoid(jnp.dot(_rmsnorm(h, gple_ref[...]).astype(BF16), wgate_ref[...],
                                    preferred_element_type=F32)) for h in hs]
    for r, h, gate, pe in zip(subs, hs, gates, pes):
        h = h + gate * pe
        o_ref[r, :] = _rmsnorm(h, gfin_ref[...]) if final else h


def _out_projection(x2d, y_moba, y_diff, p2d, w_out, w_ple, w_gate, g_ple, g_final, final):
    rows, d_model = x2d.shape
    tm = OUT_ROW_TILE
    row_spec = lambda a: pl.BlockSpec((tm, a.shape[1]), lambda i: (i, 0))
    full_spec = lambda a: pl.BlockSpec(a.shape, lambda i: (0, 0))
    return pl.pallas_call(
        functools.partial(_outproj_kernel, final=final),
        grid=(rows // tm,),
        in_specs=[row_spec(x2d), row_spec(y_moba), row_spec(y_diff), row_spec(p2d),
                  full_spec(w_out), full_spec(w_ple), full_spec(w_gate),
                  full_spec(g_ple), full_spec(g_final)],
        out_specs=pl.BlockSpec((tm, d_model), lambda i: (i, 0)),
        out_shape=jax.ShapeDtypeStruct((rows, d_model), F32),
        compiler_params=pltpu.CompilerParams(
            dimension_semantics=("arbitrary",), vmem_limit_bytes=VMEM_LIMIT_BYTES),
        name="out_projection",
    )(x2d, y_moba, y_diff, p2d, w_out, w_ple, w_gate, g_ple, g_final)


def kernel(x, p, w_in, w_out, g_mix, diff_lq1, diff_lk1, diff_lq2, diff_lk2, g_subln,
           w_ple, w_ple_gate, g_ple, g_final):
    batch, seq, d_model = x.shape
    depth = w_in.shape[0]
    rows = batch * seq
    assert w_in.shape[2] == QKV_W + MOBA_W + DIFF_W
    assert seq % ROW_TILE == 0 and ROW_TILE % MOBA_BLOCK == 0 and Q_TILE == MOBA_BLOCK
    rope_tab = _rope_tables(seq)
    row = lambda a: a.reshape(1, -1)
    h = x.reshape(rows, d_model)
    for i in range(depth):
        lambda_init = 0.8 - 0.6 * float(np.exp(-0.3 * i))
        qt, k, vt, gz, kmean = _in_projection(h, row(g_mix[i]), w_in[i].astype(BF16), rope_tab,
                                              batch, seq)
        k = k.reshape(batch, seq, 2 * SLAB)
        gz = gz.reshape(batch, seq, MOBA_W + DIFF_W)
        kmean = kmean.reshape(batch, seq // MOBA_BLOCK, SLAB)
        moba_groups = MOBA_W // LANES
        y_moba = _attention(qt, k, vt, gz, (kmean,), moba=True, lambda_init=lambda_init,
                            col=0, groups=moba_groups)
        y_diff = _attention(qt, k, vt, gz, (row(diff_lq1[i]), row(diff_lk1[i]), row(diff_lq2[i]),
                                            row(diff_lk2[i]), row(g_subln[i])),
                            moba=False, lambda_init=lambda_init,
                            col=moba_groups, groups=DIFF_W // LANES)
        h = _out_projection(h, y_moba.reshape(rows, MOBA_W), y_diff.reshape(rows, DIFF_W),
                            p[i].reshape(rows, -1), w_out[i].astype(BF16),
                            w_ple[i].astype(BF16), w_ple_gate[i].astype(BF16),
                            row(g_ple[i]), row(g_final), final=(i == depth - 1))
    return h.reshape(batch, seq, d_model)
```

```python
import functools

import numpy as np
import jax
import jax.numpy as jnp
from jax import lax
from jax.experimental import pallas as pl
from jax.experimental.pallas import tpu as pltpu

HEAD_DIM = 64
MOBA_HEADS = 8
MOBA_BLOCK = 256
MOBA_TOPK = 3
DIFF_HEADS = 4
ROPE_THETA = 500000.0
ROPE_DIM = HEAD_DIM // 4
EPS = 1e-6

LANES = 128
MOBA_W = MOBA_HEADS * HEAD_DIM
DIFF_W = DIFF_HEADS * 2 * HEAD_DIM
SLAB = 512
QKV_W = 6 * SLAB
ROW_TILE = 512
OUT_ROW_TILE = 1024
OUT_SUBTILES = 4
Q_TILE = MOBA_BLOCK
ATTN_GROUPS_PER_STEP = 2
MOBA_QK_LOOKAHEAD = 3
DIFF_QK_LOOKAHEAD = 5
VMEM_LIMIT_BYTES = 48 * 1024 * 1024

F32 = jnp.float32
BF16 = jnp.bfloat16
NEG_INF = float("-inf")
MASK_BIAS = -(2.0 ** 100)
ONES_ROWS = 16


def _rope_tables(seq):
    inv = ROPE_THETA ** (-jnp.arange(0, ROPE_DIM, 2, dtype=F32) / ROPE_DIM)
    ang = jnp.arange(seq, dtype=F32)[:, None] * inv[None, :]
    cos, sin = jnp.cos(ang), jnp.sin(ang)
    half = ROPE_DIM // 2
    ones = jnp.ones((seq, HEAD_DIM - ROPE_DIM), F32)
    zeros = lambda n: jnp.zeros((seq, n), F32)
    c = jnp.concatenate([cos, cos, ones], axis=-1)
    s1 = jnp.concatenate([-sin, zeros(HEAD_DIM - half)], axis=-1)
    s2 = jnp.concatenate([zeros(half), sin, zeros(HEAD_DIM - ROPE_DIM)], axis=-1)
    k_tabs = [jnp.concatenate([t, t], axis=-1) for t in (c, s1, s2)]
    scale = HEAD_DIM ** -0.5 * float(np.log2(np.e))
    q_tabs = [t * scale for t in k_tabs]
    return jnp.concatenate(q_tabs + k_tabs, axis=-1)


def _rmsnorm(x, g):
    return x * lax.rsqrt(jnp.mean(x * x, axis=-1, keepdims=True) + EPS) * g


def _inproj_kernel(x_ref, g_ref, w_ref, rope_ref, qt_ref, k_ref, vt_ref, gz_ref, kmean_ref):
    tm = x_ref.shape[0]

    def rope(slab, rows, tab0):
        outs = []
        for c in range(SLAB // LANES):
            t = slab[:, c * LANES:(c + 1) * LANES]
            cos = rope_ref[rows, (tab0 + 0) * LANES:(tab0 + 1) * LANES]
            s1 = rope_ref[rows, (tab0 + 1) * LANES:(tab0 + 2) * LANES]
            s2 = rope_ref[rows, (tab0 + 2) * LANES:(tab0 + 3) * LANES]
            half = ROPE_DIM // 2
            outs.append(t * cos + pltpu.roll(t, LANES - half, 1) * s1 + pltpu.roll(t, half, 1) * s2)
        return jnp.concatenate(outs, axis=-1)

    for blk in range(tm // MOBA_BLOCK):
        rows = slice(blk * MOBA_BLOCK, (blk + 1) * MOBA_BLOCK)
        u = _rmsnorm(x_ref[rows, :], g_ref[...]).astype(BF16)
        z = jnp.dot(u, w_ref[:, QKV_W:], preferred_element_type=F32)
        gz_ref[rows, :] = z * jax.nn.sigmoid(z)
        for sec in range(6):
            slab = jnp.dot(u, w_ref[:, sec * SLAB:(sec + 1) * SLAB], preferred_element_type=F32)
            head_cols = slice((sec // 3) * SLAB, (sec // 3 + 1) * SLAB)
            if sec % 3 == 0:
                qt_ref[0, head_cols, rows] = rope(slab, rows, 0).T.astype(BF16)
            elif sec % 3 == 1:
                slab = rope(slab, rows, 3)
                if sec == 1:
                    kmean_ref[blk] = jnp.mean(slab, axis=0, keepdims=True)
                k_ref[rows, head_cols] = slab.astype(BF16)
            else:
                vt_ref[0, head_cols, rows] = slab.T.astype(BF16)


def _in_projection(x2d, g_mix, w_in_bf16, rope_tab, batch, seq):
    rows, d_model = x2d.shape
    proj_w = w_in_bf16.shape[1]
    tm = ROW_TILE
    seq_tiles = seq // tm
    heads_w = 2 * SLAB
    t_spec = pl.BlockSpec((1, heads_w, tm), lambda i: (i // seq_tiles, 0, i % seq_tiles))
    return pl.pallas_call(
        _inproj_kernel,
        grid=(rows // tm,),
        in_specs=[
            pl.BlockSpec((tm, d_model), lambda i: (i, 0)),
            pl.BlockSpec((1, d_model), lambda i: (0, 0)),
            pl.BlockSpec((d_model, proj_w), lambda i: (0, 0)),
            pl.BlockSpec((tm, 6 * LANES), lambda i: (i % seq_tiles, 0)),
        ],
        out_specs=[
            t_spec,
            pl.BlockSpec((tm, heads_w), lambda i: (i, 0)),
            t_spec,
            pl.BlockSpec((tm, proj_w - QKV_W), lambda i: (i, 0)),
            pl.BlockSpec((tm // MOBA_BLOCK, 1, SLAB), lambda i: (i, 0, 0)),
        ],
        out_shape=[
            jax.ShapeDtypeStruct((batch, heads_w, seq), BF16),
            jax.ShapeDtypeStruct((rows, heads_w), BF16),
            jax.ShapeDtypeStruct((batch, heads_w, seq), BF16),
            jax.ShapeDtypeStruct((rows, proj_w - QKV_W), F32),
            jax.ShapeDtypeStruct((rows // MOBA_BLOCK, 1, SLAB), F32),
        ],
        compiler_params=pltpu.CompilerParams(
            dimension_semantics=("arbitrary",), vmem_limit_bytes=VMEM_LIMIT_BYTES),
        name="in_projection",
    )(x2d, g_mix, w_in_bf16, rope_tab)


def _attn_kernel(*refs, moba, lambda_init):
    if moba:
        qt_ref, k_ref, vt_ref, gz_ref, kmean_ref, o_ref = refs
    else:
        qt_ref, k_ref, vt_ref, gz_ref, lq1_ref, lk1_ref, lq2_ref, lk2_ref, gsub_ref, o_ref = refs
    seq = k_ref.shape[1]
    n_groups = k_ref.shape[2] // LANES
    tq = Q_TILE
    group = lambda g: slice(g * LANES, (g + 1) * LANES)

    first_rows = lax.broadcasted_iota(jnp.int32, (LANES, tq), 0) < HEAD_DIM
    own_rows = (first_rows, jnp.logical_not(first_rows))
    causal = (lax.broadcasted_iota(jnp.int32, (tq, tq), 0)
              <= lax.broadcasted_iota(jnp.int32, (tq, tq), 1))
    vt_ones = [jnp.concatenate([vt_ref[0, group(g), :], jnp.ones((ONES_ROWS, seq), BF16)], axis=0)
               for g in range(n_groups)]

    if moba:
        lane_k = lax.broadcasted_iota(jnp.int32, (seq, LANES), 1)
        key_blk = lax.broadcasted_iota(jnp.int32, (seq, LANES), 0) // MOBA_BLOCK
        first_k = lane_k < HEAD_DIM
        hot_hi = jnp.where(lane_k - HEAD_DIM == key_blk, 1.0, 0.0).astype(BF16)
        hot_lo = jnp.where(lane_k == key_blk, 1.0, 0.0).astype(BF16)
        ks = []
        for g in range(n_groups):
            k2 = k_ref[0, :, group(g)]
            ks.append((jnp.where(first_k, k2, hot_hi), jnp.where(first_k, hot_lo, k2)))
        km = kmean_ref[0]
        km_hi = km.astype(BF16)
        km_lo = (km - km_hi.astype(F32)).astype(BF16)
        nb = km.shape[0]
        blk = lax.broadcasted_iota(jnp.int32, (nb, tq), 0)
    else:
        ks = [(k_ref[0, :, group(g)],) * 2 for g in range(n_groups)]
        lam = (jnp.exp(jnp.sum(lq1_ref[...] * lk1_ref[...], axis=1, keepdims=True))
               - jnp.exp(jnp.sum(lq2_ref[...] * lk2_ref[...], axis=1, keepdims=True))
               + lambda_init)

    def block_mask(qt, j, g):
        rows = []
        for h in (1, 0):
            qh = jnp.where(own_rows[h], qt, jnp.zeros_like(qt))
            gate = (jnp.dot(km_hi[:, group(g)], qh, preferred_element_type=F32)
                    + jnp.dot(km_lo[:, group(g)], qh, preferred_element_type=F32))
            past = blk < j
            gate = jnp.where(past, gate, NEG_INF)
            beaten = jnp.zeros((nb, tq), jnp.int32)
            for m in range(j):
                gm = gate[m:m + 1, :]
                beats = (gm > gate) | ((gm == gate) & (blk > m))
                beaten = beaten + beats.astype(jnp.int32)
            dropped = past & (beaten >= MOBA_TOPK)
            rows.append(jnp.where(dropped, MASK_BIAS, 0.0).astype(F32))
            rows.append(jnp.zeros((HEAD_DIM - nb, tq), F32))
        return jnp.concatenate(rows, axis=0).astype(BF16)

    def query_side(j, g):
        qt = qt_ref[0, group(g), j * tq:(j + 1) * tq]
        other = block_mask(qt, j, g) if (moba and j > MOBA_TOPK) else jnp.zeros_like(qt)
        return [jnp.where(own_rows[h], qt, other) for h in range(2)]

    def scores(qa, j, g, h):
        return jnp.dot(ks[g][h][:(j + 1) * tq], qa, preferred_element_type=F32)

    def weights(s, j):
        length = (j + 1) * tq
        s_own = jnp.where(causal, s[length - tq:], NEG_INF)
        m = jnp.max(s_own, axis=0, keepdims=True)
        if j > 0:
            s_past = s[:length - tq]
            m = jnp.maximum(m, jnp.max(s_past, axis=0, keepdims=True))
        p = jnp.exp2(s_own - m)
        if j > 0:
            p = jnp.concatenate([jnp.exp2(s_past - m), p], axis=0)
        return p.astype(BF16)

    def finish(p_pair, j, g):
        length = (j + 1) * tq
        p_both = jnp.concatenate(p_pair, axis=1)
        acc = jnp.dot(vt_ones[g][:, :length], p_both, preferred_element_type=F32)
        o = acc[:LANES] / acc[LANES:LANES + 1]
        o0, o1 = o[:, :tq], o[:, tq:]
        if moba:
            out = jnp.where(first_rows, o0, o1).T
        else:
            out = _rmsnorm((o0 - lam * o1).T, gsub_ref[...]) * (1.0 - lambda_init)
        rows = slice(j * tq, (j + 1) * tq)
        o_ref[0, rows, group(g)] = (out * gz_ref[0, rows, group(g)]).astype(BF16)

    lookahead = MOBA_QK_LOOKAHEAD if moba else DIFF_QK_LOOKAHEAD
    units = [(j, g, h) for j in reversed(range(seq // tq)) for g in range(n_groups) for h in range(2)]
    pending = []

    def issue(idx):
        jn, gn, hn = units[idx]
        if hn == 0:
            issue.qa = query_side(jn, gn)
        pending.append(scores(issue.qa[hn], jn, gn, hn))

    for idx in range(min(lookahead, len(units))):
        issue(idx)
    p_pair = []
    for idx, (j, g, h) in enumerate(units):
        if idx + lookahead < len(units):
            issue(idx + lookahead)
        p_pair.append(weights(pending.pop(0), j))
        if h == 1:
            finish(p_pair, j, g)
            p_pair = []


def _attention(qt, k, vt, gz, extra, *, moba, lambda_init, col, groups):
    batch, seq, _ = k.shape
    per_step = ATTN_GROUPS_PER_STEP
    width = per_step * LANES
    t_spec = pl.BlockSpec((1, width, seq), lambda b, g: (b, col // per_step + g, 0))
    row_spec = pl.BlockSpec((1, seq, width), lambda b, g: (b, 0, col // per_step + g))
    in_specs = [t_spec, row_spec, t_spec, row_spec]
    if moba:
        (kmean,) = extra
        in_specs.append(pl.BlockSpec((1, kmean.shape[1], width), lambda b, g: (b, 0, g)))
    else:
        for a in extra:
            in_specs.append(pl.BlockSpec(a.shape, lambda b, g: (0, 0)))
    return pl.pallas_call(
        functools.partial(_attn_kernel, moba=moba, lambda_init=lambda_init),
        grid=(batch, groups // per_step),
        in_specs=in_specs,
        out_specs=pl.BlockSpec((1, seq, width), lambda b, g: (b, 0, g)),
        out_shape=jax.ShapeDtypeStruct((batch, seq, groups * LANES), BF16),
        compiler_params=pltpu.CompilerParams(
            dimension_semantics=("arbitrary", "arbitrary"),
            vmem_limit_bytes=VMEM_LIMIT_BYTES),
        name="moba_attention" if moba else "diff_attention",
    )(qt, k, vt, gz, *extra)


def _outproj_kernel(x_ref, ym_ref, yd_ref, p_ref, wout_ref, wple_ref, wgate_ref,
                    gple_ref, gfin_ref, o_ref, *, final):
    mw = ym_ref.shape[1]
    tm = x_ref.shape[0]
    subs = [slice(r * tm // OUT_SUBTILES, (r + 1) * tm // OUT_SUBTILES) for r in range(OUT_SUBTILES)]
    hs = [x_ref[r, :]
          + jnp.dot(ym_ref[r, :], wout_ref[:mw, :], preferred_element_type=F32)
          + jnp.dot(yd_ref[r, :], wout_ref[mw:, :], preferred_element_type=F32) for r in subs]
    pes = [jnp.dot(p_ref[r, :].astype(BF16), wple_ref[...], preferred_element_type=F32) for r in subs]
    gates = [jax.nn.sigmoid(jnp.dot(_rmsnorm(h, gple_ref[...]).astype(BF16), wgate_ref[...],
                                    preferred_element_type=F32)) for h in hs]
    for r, h, gate, pe in zip(subs, hs, gates, pes):
        h = h + gate * pe
        o_ref[r, :] = _rmsnorm(h, gfin_ref[...]) if final else h


def _out_projection(x2d, y_moba, y_diff, p2d, w_out, w_ple, w_gate, g_ple, g_final, final):
    rows, d_model = x2d.shape
    tm = OUT_ROW_TILE
    row_spec = lambda a: pl.BlockSpec((tm, a.shape[1]), lambda i: (i, 0))
    full_spec = lambda a: pl.BlockSpec(a.shape, lambda i: (0, 0))
    return pl.pallas_call(
        functools.partial(_outproj_kernel, final=final),
        grid=(rows // tm,),
        in_specs=[row_spec(x2d), row_spec(y_moba), row_spec(y_diff), row_spec(p2d),
                  full_spec(w_out), full_spec(w_ple), full_spec(w_gate),
                  full_spec(g_ple), full_spec(g_final)],
        out_specs=pl.BlockSpec((tm, d_model), lambda i: (i, 0)),
        out_shape=jax.ShapeDtypeStruct((rows, d_model), F32),
        compiler_params=pltpu.CompilerParams(
            dimension_semantics=("arbitrary",), vmem_limit_bytes=VMEM_LIMIT_BYTES),
        name="out_projection",
    )(x2d, y_moba, y_diff, p2d, w_out, w_ple, w_gate, g_ple, g_final)


def kernel(x, p, w_in, w_out, g_mix, diff_lq1, diff_lk1, diff_lq2, diff_lk2, g_subln,
           w_ple, w_ple_gate, g_ple, g_final):
    batch, seq, d_model = x.shape
    depth = w_in.shape[0]
    rows = batch * seq
    assert w_in.shape[2] == QKV_W + MOBA_W + DIFF_W
    assert seq % ROW_TILE == 0 and ROW_TILE % MOBA_BLOCK == 0 and Q_TILE == MOBA_BLOCK
    rope_tab = _rope_tables(seq)
    row = lambda a: a.reshape(1, -1)
    h = x.reshape(rows, d_model)
    for i in range(depth):
        lambda_init = 0.8 - 0.6 * float(np.exp(-0.3 * i))
        qt, k, vt, gz, kmean = _in_projection(h, row(g_mix[i]), w_in[i].astype(BF16), rope_tab,
                                              batch, seq)
        k = k.reshape(batch, seq, 2 * SLAB)
        gz = gz.reshape(batch, seq, MOBA_W + DIFF_W)
        kmean = kmean.reshape(batch, seq // MOBA_BLOCK, SLAB)
        moba_groups = MOBA_W // LANES
        y_moba = _attention(qt, k, vt, gz, (kmean,), moba=True, lambda_init=lambda_init,
                            col=0, groups=moba_groups)
        y_diff = _attention(qt, k, vt, gz, (row(diff_lq1[i]), row(diff_lk1[i]), row(diff_lq2[i]),
                                            row(diff_lk2[i]), row(g_subln[i])),
                            moba=False, lambda_init=lambda_init,
                            col=moba_groups, groups=DIFF_W // LANES)
        h = _out_projection(h, y_moba.reshape(rows, MOBA_W), y_diff.reshape(rows, DIFF_W),
                            p[i].reshape(rows, -1), w_out[i].astype(BF16),
                            w_ple[i].astype(BF16), w_ple_gate[i].astype(BF16),
                            row(g_ple[i]), row(g_final), final=(i == depth - 1))
    return h.reshape(batch, seq, d_model)
```

```python
import functools

import numpy as np
import jax
import jax.numpy as jnp
from jax import lax
from jax.experimental import pallas as pl
from jax.experimental.pallas import tpu as pltpu

HEAD_DIM = 64
MOBA_HEADS = 8
MOBA_BLOCK = 256
MOBA_TOPK = 3
DIFF_HEADS = 4
ROPE_THETA = 500000.0
ROPE_DIM = HEAD_DIM // 4
EPS = 1e-6

LANES = 128
MOBA_W = MOBA_HEADS * HEAD_DIM
DIFF_W = DIFF_HEADS * 2 * HEAD_DIM
SLAB = 512
QKV_W = 6 * SLAB
ROW_TILE = 512
OUT_ROW_TILE = 1024
OUT_SUBTILES = 4
Q_TILE = MOBA_BLOCK
ATTN_GROUPS_PER_STEP = 2
MOBA_QK_LOOKAHEAD = 3
DIFF_QK_LOOKAHEAD = 5
VMEM_LIMIT_BYTES = 48 * 1024 * 1024

F32 = jnp.float32
BF16 = jnp.bfloat16
NEG_INF = float("-inf")
MASK_BIAS = -(2.0 ** 100)
ONES_ROWS = 16


def _rope_tables(seq):
    inv = ROPE_THETA ** (-jnp.arange(0, ROPE_DIM, 2, dtype=F32) / ROPE_DIM)
    ang = jnp.arange(seq, dtype=F32)[:, None] * inv[None, :]
    cos, sin = jnp.cos(ang), jnp.sin(ang)
    half = ROPE_DIM // 2
    ones = jnp.ones((seq, HEAD_DIM - ROPE_DIM), F32)
    zeros = lambda n: jnp.zeros((seq, n), F32)
    c = jnp.concatenate([cos, cos, ones], axis=-1)
    s1 = jnp.concatenate([-sin, zeros(HEAD_DIM - half)], axis=-1)
    s2 = jnp.concatenate([zeros(half), sin, zeros(HEAD_DIM - ROPE_DIM)], axis=-1)
    k_tabs = [jnp.concatenate([t, t], axis=-1) for t in (c, s1, s2)]
    scale = HEAD_DIM ** -0.5 * float(np.log2(np.e))
    q_tabs = [t * scale for t in k_tabs]
    return jnp.concatenate(q_tabs + k_tabs, axis=-1)


def _rmsnorm(x, g):
    return x * lax.rsqrt(jnp.mean(x * x, axis=-1, keepdims=True) + EPS) * g


def _inproj_kernel(x_ref, g_ref, w_ref, rope_ref, qt_ref, k_ref, vt_ref, gz_ref, kmean_ref):
    tm = x_ref.shape[0]

    def rope(slab, rows, tab0):
        outs = []
        for c in range(SLAB // LANES):
            t = slab[:, c * LANES:(c + 1) * LANES]
            cos = rope_ref[rows, (tab0 + 0) * LANES:(tab0 + 1) * LANES]
            s1 = rope_ref[rows, (tab0 + 1) * LANES:(tab0 + 2) * LANES]
            s2 = rope_ref[rows, (tab0 + 2) * LANES:(tab0 + 3) * LANES]
            half = ROPE_DIM // 2
            outs.append(t * cos + pltpu.roll(t, LANES - half, 1) * s1 + pltpu.roll(t, half, 1) * s2)
        return jnp.concatenate(outs, axis=-1)

    for blk in range(tm // MOBA_BLOCK):
        rows = slice(blk * MOBA_BLOCK, (blk + 1) * MOBA_BLOCK)
        u = _rmsnorm(x_ref[rows, :], g_ref[...]).astype(BF16)
        z = jnp.dot(u, w_ref[:, QKV_W:], preferred_element_type=F32)
        gz_ref[rows, :] = (z * jax.nn.sigmoid(z)).astype(BF16)
        for sec in range(6):
            slab = jnp.dot(u, w_ref[:, sec * SLAB:(sec + 1) * SLAB], preferred_element_type=F32)
            head_cols = slice((sec // 3) * SLAB, (sec // 3 + 1) * SLAB)
            if sec % 3 == 0:
                qt_ref[0, head_cols, rows] = rope(slab, rows, 0).T.astype(BF16)
            elif sec % 3 == 1:
                slab = rope(slab, rows, 3)
                if sec == 1:
                    kmean_ref[blk] = jnp.mean(slab, axis=0, keepdims=True)
                k_ref[rows, head_cols] = slab.astype(BF16)
            else:
                vt_ref[0, head_cols, rows] = slab.T.astype(BF16)


def _in_projection(x2d, g_mix, w_in_bf16, rope_tab, batch, seq):
    rows, d_model = x2d.shape
    proj_w = w_in_bf16.shape[1]
    tm = ROW_TILE
    seq_tiles = seq // tm
    heads_w = 2 * SLAB
    t_spec = pl.BlockSpec((1, heads_w, tm), lambda i: (i // seq_tiles, 0, i % seq_tiles))
    return pl.pallas_call(
        _inproj_kernel,
        grid=(rows // tm,),
        in_specs=[
            pl.BlockSpec((tm, d_model), lambda i: (i, 0)),
            pl.BlockSpec((1, d_model), lambda i: (0, 0)),
            pl.BlockSpec((d_model, proj_w), lambda i: (0, 0)),
            pl.BlockSpec((tm, 6 * LANES), lambda i: (i % seq_tiles, 0)),
        ],
        out_specs=[
            t_spec,
            pl.BlockSpec((tm, heads_w), lambda i: (i, 0)),
            t_spec,
            pl.BlockSpec((tm, proj_w - QKV_W), lambda i: (i, 0)),
            pl.BlockSpec((tm // MOBA_BLOCK, 1, SLAB), lambda i: (i, 0, 0)),
        ],
        out_shape=[
            jax.ShapeDtypeStruct((batch, heads_w, seq), BF16),
            jax.ShapeDtypeStruct((rows, heads_w), BF16),
            jax.ShapeDtypeStruct((batch, heads_w, seq), BF16),
            jax.ShapeDtypeStruct((rows, proj_w - QKV_W), BF16),
            jax.ShapeDtypeStruct((rows // MOBA_BLOCK, 1, SLAB), F32),
        ],
        compiler_params=pltpu.CompilerParams(
            dimension_semantics=("arbitrary",), vmem_limit_bytes=VMEM_LIMIT_BYTES),
        name="in_projection",
    )(x2d, g_mix, w_in_bf16, rope_tab)


def _attn_kernel(*refs, moba, lambda_init):
    if moba:
        qt_ref, k_ref, vt_ref, gz_ref, kmean_ref, o_ref = refs
    else:
        qt_ref, k_ref, vt_ref, gz_ref, lq1_ref, lk1_ref, lq2_ref, lk2_ref, gsub_ref, o_ref = refs
    seq = k_ref.shape[1]
    n_groups = k_ref.shape[2] // LANES
    tq = Q_TILE
    group = lambda g: slice(g * LANES, (g + 1) * LANES)

    first_rows = lax.broadcasted_iota(jnp.int32, (LANES, tq), 0) < HEAD_DIM
    own_rows = (first_rows, jnp.logical_not(first_rows))
    causal = (lax.broadcasted_iota(jnp.int32, (tq, tq), 0)
              <= lax.broadcasted_iota(jnp.int32, (tq, tq), 1))
    vt_ones = [jnp.concatenate([vt_ref[0, group(g), :], jnp.ones((ONES_ROWS, seq), BF16)], axis=0)
               for g in range(n_groups)]

    if moba:
        lane_k = lax.broadcasted_iota(jnp.int32, (seq, LANES), 1)
        key_blk = lax.broadcasted_iota(jnp.int32, (seq, LANES), 0) // MOBA_BLOCK
        first_k = lane_k < HEAD_DIM
        hot_hi = jnp.where(lane_k - HEAD_DIM == key_blk, 1.0, 0.0).astype(BF16)
        hot_lo = jnp.where(lane_k == key_blk, 1.0, 0.0).astype(BF16)
        ks = []
        for g in range(n_groups):
            k2 = k_ref[0, :, group(g)]
            ks.append((jnp.where(first_k, k2, hot_hi), jnp.where(first_k, hot_lo, k2)))
        km = kmean_ref[0]
        km_hi = km.astype(BF16)
        km_lo = (km - km_hi.astype(F32)).astype(BF16)
        nb = km.shape[0]
        blk = lax.broadcasted_iota(jnp.int32, (nb, tq), 0)
    else:
        ks = [(k_ref[0, :, group(g)],) * 2 for g in range(n_groups)]
        lam = (jnp.exp(jnp.sum(lq1_ref[...] * lk1_ref[...], axis=1, keepdims=True))
               - jnp.exp(jnp.sum(lq2_ref[...] * lk2_ref[...], axis=1, keepdims=True))
               + lambda_init)

    def block_mask(qt, j, g):
        rows = []
        for h in (1, 0):
            qh = jnp.where(own_rows[h], qt, jnp.zeros_like(qt))
            gate = (jnp.dot(km_hi[:, group(g)], qh, preferred_element_type=F32)
                    + jnp.dot(km_lo[:, group(g)], qh, preferred_element_type=F32))
            past = blk < j
            gate = jnp.where(past, gate, NEG_INF)
            beaten = jnp.zeros((nb, tq), jnp.int32)
            for m in range(j):
                gm = gate[m:m + 1, :]
                beats = (gm > gate) | ((gm == gate) & (blk > m))
                beaten = beaten + beats.astype(jnp.int32)
            dropped = past & (beaten >= MOBA_TOPK)
            rows.append(jnp.where(dropped, MASK_BIAS, 0.0).astype(F32))
            rows.append(jnp.zeros((HEAD_DIM - nb, tq), F32))
        return jnp.concatenate(rows, axis=0).astype(BF16)

    def query_side(j, g):
        qt = qt_ref[0, group(g), j * tq:(j + 1) * tq]
        other = block_mask(qt, j, g) if (moba and j > MOBA_TOPK) else jnp.zeros_like(qt)
        return [jnp.where(own_rows[h], qt, other) for h in range(2)]

    def scores(qa, j, g, h):
        return jnp.dot(ks[g][h][:(j + 1) * tq], qa, preferred_element_type=F32)

    def weights(s, j):
        length = (j + 1) * tq
        s_own = jnp.where(causal, s[length - tq:], NEG_INF)
        m = jnp.max(s_own, axis=0, keepdims=True)
        if j > 0:
            s_past = s[:length - tq]
            m = jnp.maximum(m, jnp.max(s_past, axis=0, keepdims=True))
        p = jnp.exp2(s_own - m)
        if j > 0:
            p = jnp.concatenate([jnp.exp2(s_past - m), p], axis=0)
        return p.astype(BF16)

    def finish(p_pair, j, g):
        length = (j + 1) * tq
        p_both = jnp.concatenate(p_pair, axis=1)
        acc = jnp.dot(vt_ones[g][:, :length], p_both, preferred_element_type=F32)
        o = acc[:LANES] / acc[LANES:LANES + 1]
        o0, o1 = o[:, :tq], o[:, tq:]
        if moba:
            out = jnp.where(first_rows, o0, o1).T
        else:
            out = _rmsnorm((o0 - lam * o1).T, gsub_ref[...]) * (1.0 - lambda_init)
        rows = slice(j * tq, (j + 1) * tq)
        o_ref[0, rows, group(g)] = (out * gz_ref[0, rows, group(g)]).astype(BF16)

    lookahead = MOBA_QK_LOOKAHEAD if moba else DIFF_QK_LOOKAHEAD
    units = [(j, g, h) for j in reversed(range(seq // tq)) for g in range(n_groups) for h in range(2)]
    pending = []

    def issue(idx):
        jn, gn, hn = units[idx]
        if hn == 0:
            issue.qa = query_side(jn, gn)
        pending.append(scores(issue.qa[hn], jn, gn, hn))

    for idx in range(min(lookahead, len(units))):
        issue(idx)
    p_pair = []
    for idx, (j, g, h) in enumerate(units):
        if idx + lookahead < len(units):
            issue(idx + lookahead)
        p_pair.append(weights(pending.pop(0), j))
        if h == 1:
            finish(p_pair, j, g)
            p_pair = []


def _attention(qt, k, vt, gz, extra, *, moba, lambda_init, col, groups):
    batch, seq, _ = k.shape
    per_step = ATTN_GROUPS_PER_STEP
    width = per_step * LANES
    t_spec = pl.BlockSpec((1, width, seq), lambda b, g: (b, col // per_step + g, 0))
    row_spec = pl.BlockSpec((1, seq, width), lambda b, g: (b, 0, col // per_step + g))
    in_specs = [t_spec, row_spec, t_spec, row_spec]
    if moba:
        (kmean,) = extra
        in_specs.append(pl.BlockSpec((1, kmean.shape[1], width), lambda b, g: (b, 0, g)))
    else:
        for a in extra:
            in_specs.append(pl.BlockSpec(a.shape, lambda b, g: (0, 0)))
    return pl.pallas_call(
        functools.partial(_attn_kernel, moba=moba, lambda_init=lambda_init),
        grid=(batch, groups // per_step),
        in_specs=in_specs,
        out_specs=pl.BlockSpec((1, seq, width), lambda b, g: (b, 0, g)),
        out_shape=jax.ShapeDtypeStruct((batch, seq, groups * LANES), BF16),
        compiler_params=pltpu.CompilerParams(
            dimension_semantics=("arbitrary", "arbitrary"),
            vmem_limit_bytes=VMEM_LIMIT_BYTES),
        name="moba_attention" if moba else "diff_attention",
    )(qt, k, vt, gz, *extra)


def _outproj_kernel(x_ref, ym_ref, yd_ref, p_ref, wout_ref, wple_ref, wgate_ref,
                    gple_ref, gfin_ref, o_ref, *, final):
    mw = ym_ref.shape[1]
    tm = x_ref.shape[0]
    subs = [slice(r * tm // OUT_SUBTILES, (r + 1) * tm // OUT_SUBTILES) for r in range(OUT_SUBTILES)]
    hs = [x_ref[r, :]
          + jnp.dot(ym_ref[r, :], wout_ref[:mw, :], preferred_element_type=F32)
          + jnp.dot(yd_ref[r, :], wout_ref[mw:, :], preferred_element_type=F32) for r in subs]
    pes = [jnp.dot(p_ref[r, :].astype(BF16), wple_ref[...], preferred_element_type=F32) for r in subs]
    gates = [jax.nn.sigmoid(jnp.dot(_rmsnorm(h, gple_ref[...]).astype(BF16), wgate_ref[...],
                                    preferred_element_type=F32)) for h in hs]
    for r, h, gate, pe in zip(subs, hs, gates, pes):
        h = h + gate * pe
        o_ref[r, :] = _rmsnorm(h, gfin_ref[...]) if final else h


def _out_projection(x2d, y_moba, y_diff, p2d, w_out, w_ple, w_gate, g_ple, g_final, final):
    rows, d_model = x2d.shape
    tm = OUT_ROW_TILE
    row_spec = lambda a: pl.BlockSpec((tm, a.shape[1]), lambda i: (i, 0))
    full_spec = lambda a: pl.BlockSpec(a.shape, lambda i: (0, 0))
    return pl.pallas_call(
        functools.partial(_outproj_kernel, final=final),
        grid=(rows // tm,),
        in_specs=[row_spec(x2d), row_spec(y_moba), row_spec(y_diff), row_spec(p2d),
                  full_spec(w_out), full_spec(w_ple), full_spec(w_gate),
                  full_spec(g_ple), full_spec(g_final)],
        out_specs=pl.BlockSpec((tm, d_model), lambda i: (i, 0)),
        out_shape=jax.ShapeDtypeStruct((rows, d_model), F32),
        compiler_params=pltpu.CompilerParams(
            dimension_semantics=("arbitrary",), vmem_limit_bytes=VMEM_LIMIT_BYTES),
        name="out_projection",
    )(x2d, y_moba, y_diff, p2d, w_out, w_ple, w_gate, g_ple, g_final)


def kernel(x, p, w_in, w_out, g_mix, diff_lq1, diff_lk1, diff_lq2, diff_lk2, g_subln,
           w_ple, w_ple_gate, g_ple, g_final):
    batch, seq, d_model = x.shape
    depth = w_in.shape[0]
    rows = batch * seq
    assert w_in.shape[2] == QKV_W + MOBA_W + DIFF_W
    assert seq % ROW_TILE == 0 and ROW_TILE % MOBA_BLOCK == 0 and Q_TILE == MOBA_BLOCK
    rope_tab = _rope_tables(seq)
    row = lambda a: a.reshape(1, -1)
    h = x.reshape(rows, d_model)
    for i in range(depth):
        lambda_init = 0.8 - 0.6 * float(np.exp(-0.3 * i))
        qt, k, vt, gz, kmean = _in_projection(h, row(g_mix[i]), w_in[i].astype(BF16), rope_tab,
                                              batch, seq)
        k = k.reshape(batch, seq, 2 * SLAB)
        gz = gz.reshape(batch, seq, MOBA_W + DIFF_W)
        kmean = kmean.reshape(batch, seq // MOBA_BLOCK, SLAB)
        moba_groups = MOBA_W // LANES
        y_moba = _attention(qt, k, vt, gz, (kmean,), moba=True, lambda_init=lambda_init,
                            col=0, groups=moba_groups)
        y_diff = _attention(qt, k, vt, gz, (row(diff_lq1[i]), row(diff_lk1[i]), row(diff_lq2[i]),
                                            row(diff_lk2[i]), row(g_subln[i])),
                            moba=False, lambda_init=lambda_init,
                            col=moba_groups, groups=DIFF_W // LANES)
        h = _out_projection(h, y_moba.reshape(rows, MOBA_W), y_diff.reshape(rows, DIFF_W),
                            p[i].reshape(rows, -1), w_out[i].astype(BF16),
                            w_ple[i].astype(BF16), w_ple_gate[i].astype(BF16),
                            row(g_ple[i]), row(g_final), final=(i == depth - 1))
    return h.reshape(batch, seq, d_model)
```

```python
import functools

import numpy as np
import jax
import jax.numpy as jnp
from jax import lax
from jax.experimental import pallas as pl
from jax.experimental.pallas import tpu as pltpu

HEAD_DIM = 64
MOBA_HEADS = 8
MOBA_BLOCK = 256
MOBA_TOPK = 3
DIFF_HEADS = 4
ROPE_THETA = 500000.0
ROPE_DIM = HEAD_DIM // 4
EPS = 1e-6

LANES = 128
MOBA_W = MOBA_HEADS * HEAD_DIM
DIFF_W = DIFF_HEADS * 2 * HEAD_DIM
SLAB = 512
QKV_W = 6 * SLAB
ROW_TILE = 1024
OUT_ROW_TILE = 1024
OUT_SUBTILES = 4
Q_TILE = MOBA_BLOCK
ATTN_GROUPS_PER_STEP = 4
MOBA_QK_LOOKAHEAD = 3
DIFF_QK_LOOKAHEAD = 5
VMEM_LIMIT_BYTES = 48 * 1024 * 1024

F32 = jnp.float32
BF16 = jnp.bfloat16
NEG_INF = float("-inf")
MASK_BIAS = -(2.0 ** 100)
ONES_ROWS = 16


def _rope_tables(seq):
    inv = ROPE_THETA ** (-jnp.arange(0, ROPE_DIM, 2, dtype=F32) / ROPE_DIM)
    ang = jnp.arange(seq, dtype=F32)[:, None] * inv[None, :]
    cos, sin = jnp.cos(ang), jnp.sin(ang)
    half = ROPE_DIM // 2
    ones = jnp.ones((seq, HEAD_DIM - ROPE_DIM), F32)
    zeros = lambda n: jnp.zeros((seq, n), F32)
    c = jnp.concatenate([cos, cos, ones], axis=-1)
    s1 = jnp.concatenate([-sin, zeros(HEAD_DIM - half)], axis=-1)
    s2 = jnp.concatenate([zeros(half), sin, zeros(HEAD_DIM - ROPE_DIM)], axis=-1)
    k_tabs = [jnp.concatenate([t, t], axis=-1) for t in (c, s1, s2)]
    scale = HEAD_DIM ** -0.5 * float(np.log2(np.e))
    q_tabs = [t * scale for t in k_tabs]
    return jnp.concatenate(q_tabs + k_tabs, axis=-1)


def _rmsnorm(x, g):
    return x * lax.rsqrt(jnp.mean(x * x, axis=-1, keepdims=True) + EPS) * g


def _inproj_kernel(x_ref, g_ref, w_ref, rope_ref, qt_ref, k_ref, vt_ref, gz_ref, kmean_ref):
    tm = x_ref.shape[0]

    def rope(slab, rows, tab0):
        outs = []
        for c in range(SLAB // LANES):
            t = slab[:, c * LANES:(c + 1) * LANES]
            cos = rope_ref[rows, (tab0 + 0) * LANES:(tab0 + 1) * LANES]
            s1 = rope_ref[rows, (tab0 + 1) * LANES:(tab0 + 2) * LANES]
            s2 = rope_ref[rows, (tab0 + 2) * LANES:(tab0 + 3) * LANES]
            half = ROPE_DIM // 2
            outs.append(t * cos + pltpu.roll(t, LANES - half, 1) * s1 + pltpu.roll(t, half, 1) * s2)
        return jnp.concatenate(outs, axis=-1)

    for blk in range(tm // MOBA_BLOCK):
        rows = slice(blk * MOBA_BLOCK, (blk + 1) * MOBA_BLOCK)
        u = _rmsnorm(x_ref[rows, :], g_ref[...]).astype(BF16)
        z = jnp.dot(u, w_ref[:, QKV_W:], preferred_element_type=F32)
        gz_ref[rows, :] = z * jax.nn.sigmoid(z)
        for sec in range(6):
            slab = jnp.dot(u, w_ref[:, sec * SLAB:(sec + 1) * SLAB], preferred_element_type=F32)
            head_cols = slice((sec // 3) * SLAB, (sec // 3 + 1) * SLAB)
            if sec % 3 == 0:
                qt_ref[0, head_cols, rows] = rope(slab, rows, 0).T.astype(BF16)
            elif sec % 3 == 1:
                slab = rope(slab, rows, 3)
                if sec == 1:
                    kmean_ref[blk] = jnp.mean(slab, axis=0, keepdims=True)
                k_ref[rows, head_cols] = slab.astype(BF16)
            else:
                vt_ref[0, head_cols, rows] = slab.T.astype(BF16)


def _in_projection(x2d, g_mix, w_in_bf16, rope_tab, batch, seq):
    rows, d_model = x2d.shape
    proj_w = w_in_bf16.shape[1]
    tm = ROW_TILE
    seq_tiles = seq // tm
    heads_w = 2 * SLAB
    t_spec = pl.BlockSpec((1, heads_w, tm), lambda i: (i // seq_tiles, 0, i % seq_tiles))
    return pl.pallas_call(
        _inproj_kernel,
        grid=(rows // tm,),
        in_specs=[
            pl.BlockSpec((tm, d_model), lambda i: (i, 0)),
            pl.BlockSpec((1, d_model), lambda i: (0, 0)),
            pl.BlockSpec((d_model, proj_w), lambda i: (0, 0), pipeline_mode=pl.Buffered(1)),
            pl.BlockSpec((tm, 6 * LANES), lambda i: (i % seq_tiles, 0)),
        ],
        out_specs=[
            t_spec,
            pl.BlockSpec((tm, heads_w), lambda i: (i, 0)),
            t_spec,
            pl.BlockSpec((tm, proj_w - QKV_W), lambda i: (i, 0)),
            pl.BlockSpec((tm // MOBA_BLOCK, 1, SLAB), lambda i: (i, 0, 0)),
        ],
        out_shape=[
            jax.ShapeDtypeStruct((batch, heads_w, seq), BF16),
            jax.ShapeDtypeStruct((rows, heads_w), BF16),
            jax.ShapeDtypeStruct((batch, heads_w, seq), BF16),
            jax.ShapeDtypeStruct((rows, proj_w - QKV_W), F32),
            jax.ShapeDtypeStruct((rows // MOBA_BLOCK, 1, SLAB), F32),
        ],
        compiler_params=pltpu.CompilerParams(
            dimension_semantics=("arbitrary",), vmem_limit_bytes=VMEM_LIMIT_BYTES),
        name="in_projection",
    )(x2d, g_mix, w_in_bf16, rope_tab)


def _attn_kernel(*refs, moba, lambda_init):
    if moba:
        qt_ref, k_ref, vt_ref, gz_ref, kmean_ref, o_ref = refs
    else:
        qt_ref, k_ref, vt_ref, gz_ref, lq1_ref, lk1_ref, lq2_ref, lk2_ref, gsub_ref, o_ref = refs
    seq = k_ref.shape[1]
    n_groups = k_ref.shape[2] // LANES
    tq = Q_TILE
    group = lambda g: slice(g * LANES, (g + 1) * LANES)

    first_rows = lax.broadcasted_iota(jnp.int32, (LANES, tq), 0) < HEAD_DIM
    own_rows = (first_rows, jnp.logical_not(first_rows))
    causal = (lax.broadcasted_iota(jnp.int32, (tq, tq), 0)
              <= lax.broadcasted_iota(jnp.int32, (tq, tq), 1))
    vt_ones = [jnp.concatenate([vt_ref[0, group(g), :], jnp.ones((ONES_ROWS, seq), BF16)], axis=0)
               for g in range(n_groups)]

    if moba:
        lane_k = lax.broadcasted_iota(jnp.int32, (seq, LANES), 1)
        key_blk = lax.broadcasted_iota(jnp.int32, (seq, LANES), 0) // MOBA_BLOCK
        first_k = lane_k < HEAD_DIM
        hot_hi = jnp.where(lane_k - HEAD_DIM == key_blk, 1.0, 0.0).astype(BF16)
        hot_lo = jnp.where(lane_k == key_blk, 1.0, 0.0).astype(BF16)
        ks = []
        for g in range(n_groups):
            k2 = k_ref[0, :, group(g)]
            ks.append((jnp.where(first_k, k2, hot_hi), jnp.where(first_k, hot_lo, k2)))
        km = kmean_ref[0]
        km_hi = km.astype(BF16)
        km_lo = (km - km_hi.astype(F32)).astype(BF16)
        nb = km.shape[0]
        blk = lax.broadcasted_iota(jnp.int32, (nb, tq), 0)
    else:
        ks = [(k_ref[0, :, group(g)],) * 2 for g in range(n_groups)]
        lam = (jnp.exp(jnp.sum(lq1_ref[...] * lk1_ref[...], axis=1, keepdims=True))
               - jnp.exp(jnp.sum(lq2_ref[...] * lk2_ref[...], axis=1, keepdims=True))
               + lambda_init)

    def block_mask(qt, j, g):
        rows = []
        for h in (1, 0):
            qh = jnp.where(own_rows[h], qt, jnp.zeros_like(qt))
            gate = (jnp.dot(km_hi[:, group(g)], qh, preferred_element_type=F32)
                    + jnp.dot(km_lo[:, group(g)], qh, preferred_element_type=F32))
            past = blk < j
            gate = jnp.where(past, gate, NEG_INF)
            beaten = jnp.zeros((nb, tq), jnp.int32)
            for m in range(j):
                gm = gate[m:m + 1, :]
                beats = (gm > gate) | ((gm == gate) & (blk > m))
                beaten = beaten + beats.astype(jnp.int32)
            dropped = past & (beaten >= MOBA_TOPK)
            rows.append(jnp.where(dropped, MASK_BIAS, 0.0).astype(F32))
            rows.append(jnp.zeros((HEAD_DIM - nb, tq), F32))
        return jnp.concatenate(rows, axis=0).astype(BF16)

    def query_side(j, g):
        qt = qt_ref[0, group(g), j * tq:(j + 1) * tq]
        other = block_mask(qt, j, g) if (moba and j > MOBA_TOPK) else jnp.zeros_like(qt)
        return [jnp.where(own_rows[h], qt, other) for h in range(2)]

    def scores(qa, j, g, h):
        return jnp.dot(ks[g][h][:(j + 1) * tq], qa, preferred_element_type=F32)

    def weights(s, j):
        length = (j + 1) * tq
        s_own = jnp.where(causal, s[length - tq:], NEG_INF)
        m = jnp.max(s_own, axis=0, keepdims=True)
        if j > 0:
            s_past = s[:length - tq]
            m = jnp.maximum(m, jnp.max(s_past, axis=0, keepdims=True))
        p = jnp.exp2(s_own - m)
        if j > 0:
            p = jnp.concatenate([jnp.exp2(s_past - m), p], axis=0)
        return p.astype(BF16)

    def finish(p_pair, j, g):
        length = (j + 1) * tq
        p_both = jnp.concatenate(p_pair, axis=1)
        acc = jnp.dot(vt_ones[g][:, :length], p_both, preferred_element_type=F32)
        o = acc[:LANES] / acc[LANES:LANES + 1]
        o0, o1 = o[:, :tq], o[:, tq:]
        if moba:
            out = jnp.where(first_rows, o0, o1).T
        else:
            out = _rmsnorm((o0 - lam * o1).T, gsub_ref[...]) * (1.0 - lambda_init)
        rows = slice(j * tq, (j + 1) * tq)
        o_ref[0, rows, group(g)] = (out * gz_ref[0, rows, group(g)]).astype(BF16)

    lookahead = MOBA_QK_LOOKAHEAD if moba else DIFF_QK_LOOKAHEAD
    units = [(j, g, h) for j in reversed(range(seq // tq)) for g in range(n_groups) for h in range(2)]
    pending = []

    def issue(idx):
        jn, gn, hn = units[idx]
        if hn == 0:
            issue.qa = query_side(jn, gn)
        pending.append(scores(issue.qa[hn], jn, gn, hn))

    for idx in range(min(lookahead, len(units))):
        issue(idx)
    p_pair = []
    for idx, (j, g, h) in enumerate(units):
        if idx + lookahead < len(units):
            issue(idx + lookahead)
        p_pair.append(weights(pending.pop(0), j))
        if h == 1:
            finish(p_pair, j, g)
            p_pair = []


def _attention(qt, k, vt, gz, extra, *, moba, lambda_init, col, groups):
    batch, seq, _ = k.shape
    per_step = ATTN_GROUPS_PER_STEP
    width = per_step * LANES
    t_spec = pl.BlockSpec((1, width, seq), lambda b, g: (b, col // per_step + g, 0))
    row_spec = pl.BlockSpec((1, seq, width), lambda b, g: (b, 0, col // per_step + g))
    in_specs = [t_spec, row_spec, t_spec, row_spec]
    if moba:
        (kmean,) = extra
        in_specs.append(pl.BlockSpec((1, kmean.shape[1], width), lambda b, g: (b, 0, g)))
    else:
        for a in extra:
            in_specs.append(pl.BlockSpec(a.shape, lambda b, g: (0, 0)))
    return pl.pallas_call(
        functools.partial(_attn_kernel, moba=moba, lambda_init=lambda_init),
        grid=(batch, groups // per_step),
        in_specs=in_specs,
        out_specs=pl.BlockSpec((1, seq, width), lambda b, g: (b, 0, g)),
        out_shape=jax.ShapeDtypeStruct((batch, seq, groups * LANES), BF16),
        compiler_params=pltpu.CompilerParams(
            dimension_semantics=("arbitrary", "arbitrary"),
            vmem_limit_bytes=VMEM_LIMIT_BYTES),
        name="moba_attention" if moba else "diff_attention",
    )(qt, k, vt, gz, *extra)


def _outproj_kernel(x_ref, ym_ref, yd_ref, p_ref, wout_ref, wple_ref, wgate_ref,
                    gple_ref, gfin_ref, o_ref, *, final):
    mw = ym_ref.shape[1]
    tm = x_ref.shape[0]
    subs = [slice(r * tm // OUT_SUBTILES, (r + 1) * tm // OUT_SUBTILES) for r in range(OUT_SUBTILES)]
    hs = [x_ref[r, :]
          + jnp.dot(ym_ref[r, :], wout_ref[:mw, :], preferred_element_type=F32)
          + jnp.dot(yd_ref[r, :], wout_ref[mw:, :], preferred_element_type=F32) for r in subs]
    pes = [jnp.dot(p_ref[r, :].astype(BF16), wple_ref[...], preferred_element_type=F32) for r in subs]
    gates = [jax.nn.sigmoid(jnp.dot(_rmsnorm(h, gple_ref[...]).astype(BF16), wgate_ref[...],
                                    preferred_element_type=F32)) for h in hs]
    for r, h, gate, pe in zip(subs, hs, gates, pes):
        h = h + gate * pe
        o_ref[r, :] = _rmsnorm(h, gfin_ref[...]) if final else h


def _out_projection(x2d, y_moba, y_diff, p2d, w_out, w_ple, w_gate, g_ple, g_final, final):
    rows, d_model = x2d.shape
    tm = OUT_ROW_TILE
    row_spec = lambda a: pl.BlockSpec((tm, a.shape[1]), lambda i: (i, 0))
    full_spec = lambda a: pl.BlockSpec(a.shape, lambda i: (0, 0))
    return pl.pallas_call(
        functools.partial(_outproj_kernel, final=final),
        grid=(rows // tm,),
        in_specs=[row_spec(x2d), row_spec(y_moba), row_spec(y_diff), row_spec(p2d),
                  full_spec(w_out), full_spec(w_ple), full_spec(w_gate),
                  full_spec(g_ple), full_spec(g_final)],
        out_specs=pl.BlockSpec((tm, d_model), lambda i: (i, 0)),
        out_shape=jax.ShapeDtypeStruct((rows, d_model), F32),
        compiler_params=pltpu.CompilerParams(
            dimension_semantics=("arbitrary",), vmem_limit_bytes=VMEM_LIMIT_BYTES),
        name="out_projection",
    )(x2d, y_moba, y_diff, p2d, w_out, w_ple, w_gate, g_ple, g_final)


def kernel(x, p, w_in, w_out, g_mix, diff_lq1, diff_lk1, diff_lq2, diff_lk2, g_subln,
           w_ple, w_ple_gate, g_ple, g_final):
    batch, seq, d_model = x.shape
    depth = w_in.shape[0]
    rows = batch * seq
    assert w_in.shape[2] == QKV_W + MOBA_W + DIFF_W
    assert seq % ROW_TILE == 0 and ROW_TILE % MOBA_BLOCK == 0 and Q_TILE == MOBA_BLOCK
    rope_tab = _rope_tables(seq)
    row = lambda a: a.reshape(1, -1)
    h = x.reshape(rows, d_model)
    for i in range(depth):
        lambda_init = 0.8 - 0.6 * float(np.exp(-0.3 * i))
        qt, k, vt, gz, kmean = _in_projection(h, row(g_mix[i]), w_in[i].astype(BF16), rope_tab,
                                              batch, seq)
        k = k.reshape(batch, seq, 2 * SLAB)
        gz = gz.reshape(batch, seq, MOBA_W + DIFF_W)
        kmean = kmean.reshape(batch, seq // MOBA_BLOCK, SLAB)
        moba_groups = MOBA_W // LANES
        y_moba = _attention(qt, k, vt, gz, (kmean,), moba=True, lambda_init=lambda_init,
                            col=0, groups=moba_groups)
        y_diff = _attention(qt, k, vt, gz, (row(diff_lq1[i]), row(diff_lk1[i]), row(diff_lq2[i]),
                                            row(diff_lk2[i]), row(g_subln[i])),
                            moba=False, lambda_init=lambda_init,
                            col=moba_groups, groups=DIFF_W // LANES)
        h = _out_projection(h, y_moba.reshape(rows, MOBA_W), y_diff.reshape(rows, DIFF_W),
                            p[i].reshape(rows, -1), w_out[i].astype(BF16),
                            w_ple[i].astype(BF16), w_ple_gate[i].astype(BF16),
                            row(g_ple[i]), row(g_final), final=(i == depth - 1))
    return h.reshape(batch, seq, d_model)
```

```python
import functools

import numpy as np
import jax
import jax.numpy as jnp
from jax import lax
from jax.experimental import pallas as pl
from jax.experimental.pallas import tpu as pltpu

HEAD_DIM = 64
MOBA_HEADS = 8
MOBA_BLOCK = 256
MOBA_TOPK = 3
DIFF_HEADS = 4
ROPE_THETA = 500000.0
ROPE_DIM = HEAD_DIM // 4
EPS = 1e-6

LANES = 128
MOBA_W = MOBA_HEADS * HEAD_DIM
DIFF_W = DIFF_HEADS * 2 * HEAD_DIM
SLAB = 512
QKV_W = 6 * SLAB
ROW_TILE = 1024
OUT_ROW_TILE = 1024
OUT_SUBTILES = 4
Q_TILE = MOBA_BLOCK
MOBA_GROUPS_PER_STEP = 4
DIFF_GROUPS_PER_STEP = 2
MOBA_QK_LOOKAHEAD = 3
DIFF_QK_LOOKAHEAD = 5
VMEM_LIMIT_BYTES = 48 * 1024 * 1024

F32 = jnp.float32
BF16 = jnp.bfloat16
NEG_INF = float("-inf")
MASK_BIAS = -(2.0 ** 100)
ONES_ROWS = 16


def _rope_tables(seq):
    inv = ROPE_THETA ** (-jnp.arange(0, ROPE_DIM, 2, dtype=F32) / ROPE_DIM)
    ang = jnp.arange(seq, dtype=F32)[:, None] * inv[None, :]
    cos, sin = jnp.cos(ang), jnp.sin(ang)
    half = ROPE_DIM // 2
    ones = jnp.ones((seq, HEAD_DIM - ROPE_DIM), F32)
    zeros = lambda n: jnp.zeros((seq, n), F32)
    c = jnp.concatenate([cos, cos, ones], axis=-1)
    s1 = jnp.concatenate([-sin, zeros(HEAD_DIM - half)], axis=-1)
    s2 = jnp.concatenate([zeros(half), sin, zeros(HEAD_DIM - ROPE_DIM)], axis=-1)
    k_tabs = [jnp.concatenate([t, t], axis=-1) for t in (c, s1, s2)]
    scale = HEAD_DIM ** -0.5 * float(np.log2(np.e))
    q_tabs = [t * scale for t in k_tabs]
    return jnp.concatenate(q_tabs + k_tabs, axis=-1)


def _rmsnorm(x, g):
    return x * lax.rsqrt(jnp.mean(x * x, axis=-1, keepdims=True) + EPS) * g


def _inproj_kernel(x_ref, g_ref, w_ref, rope_ref, qt_ref, k_ref, vt_ref, gz_ref, kmean_ref):
    tm = x_ref.shape[0]

    def rope(slab, rows, tab0):
        outs = []
        for c in range(SLAB // LANES):
            t = slab[:, c * LANES:(c + 1) * LANES]
            cos = rope_ref[rows, (tab0 + 0) * LANES:(tab0 + 1) * LANES]
            s1 = rope_ref[rows, (tab0 + 1) * LANES:(tab0 + 2) * LANES]
            s2 = rope_ref[rows, (tab0 + 2) * LANES:(tab0 + 3) * LANES]
            half = ROPE_DIM // 2
            outs.append(t * cos + pltpu.roll(t, LANES - half, 1) * s1 + pltpu.roll(t, half, 1) * s2)
        return jnp.concatenate(outs, axis=-1)

    for blk in range(tm // MOBA_BLOCK):
        rows = slice(blk * MOBA_BLOCK, (blk + 1) * MOBA_BLOCK)
        u = _rmsnorm(x_ref[rows, :], g_ref[...]).astype(BF16)
        z = jnp.dot(u, w_ref[:, QKV_W:], preferred_element_type=F32)
        gz_ref[rows, :] = z * jax.nn.sigmoid(z)
        for sec in range(6):
            slab = jnp.dot(u, w_ref[:, sec * SLAB:(sec + 1) * SLAB], preferred_element_type=F32)
            head_cols = slice((sec // 3) * SLAB, (sec // 3 + 1) * SLAB)
            if sec % 3 == 0:
                qt_ref[0, head_cols, rows] = rope(slab, rows, 0).T.astype(BF16)
            elif sec % 3 == 1:
                slab = rope(slab, rows, 3)
                if sec == 1:
                    kmean_ref[blk] = jnp.mean(slab, axis=0, keepdims=True)
                k_ref[rows, head_cols] = slab.astype(BF16)
            else:
                vt_ref[0, head_cols, rows] = slab.T.astype(BF16)


def _in_projection(x2d, g_mix, w_in_bf16, rope_tab, batch, seq):
    rows, d_model = x2d.shape
    proj_w = w_in_bf16.shape[1]
    tm = ROW_TILE
    seq_tiles = seq // tm
    heads_w = 2 * SLAB
    t_spec = pl.BlockSpec((1, heads_w, tm), lambda i: (i // seq_tiles, 0, i % seq_tiles))
    return pl.pallas_call(
        _inproj_kernel,
        grid=(rows // tm,),
        in_specs=[
            pl.BlockSpec((tm, d_model), lambda i: (i, 0)),
            pl.BlockSpec((1, d_model), lambda i: (0, 0)),
            pl.BlockSpec((d_model, proj_w), lambda i: (0, 0), pipeline_mode=pl.Buffered(1)),
            pl.BlockSpec((tm, 6 * LANES), lambda i: (i % seq_tiles, 0)),
        ],
        out_specs=[
            t_spec,
            pl.BlockSpec((tm, heads_w), lambda i: (i, 0)),
            t_spec,
            pl.BlockSpec((tm, proj_w - QKV_W), lambda i: (i, 0)),
            pl.BlockSpec((tm // MOBA_BLOCK, 1, SLAB), lambda i: (i, 0, 0)),
        ],
        out_shape=[
            jax.ShapeDtypeStruct((batch, heads_w, seq), BF16),
            jax.ShapeDtypeStruct((rows, heads_w), BF16),
            jax.ShapeDtypeStruct((batch, heads_w, seq), BF16),
            jax.ShapeDtypeStruct((rows, proj_w - QKV_W), F32),
            jax.ShapeDtypeStruct((rows // MOBA_BLOCK, 1, SLAB), F32),
        ],
        compiler_params=pltpu.CompilerParams(
            dimension_semantics=("arbitrary",), vmem_limit_bytes=VMEM_LIMIT_BYTES),
        name="in_projection",
    )(x2d, g_mix, w_in_bf16, rope_tab)


def _attn_kernel(*refs, moba, lambda_init):
    if moba:
        qt_ref, k_ref, vt_ref, gz_ref, kmean_ref, o_ref = refs
    else:
        qt_ref, k_ref, vt_ref, gz_ref, lq1_ref, lk1_ref, lq2_ref, lk2_ref, gsub_ref, o_ref = refs
    seq = k_ref.shape[1]
    n_groups = k_ref.shape[2] // LANES
    tq = Q_TILE
    group = lambda g: slice(g * LANES, (g + 1) * LANES)

    first_rows = lax.broadcasted_iota(jnp.int32, (LANES, tq), 0) < HEAD_DIM
    own_rows = (first_rows, jnp.logical_not(first_rows))
    causal = (lax.broadcasted_iota(jnp.int32, (tq, tq), 0)
              <= lax.broadcasted_iota(jnp.int32, (tq, tq), 1))
    vt_ones = [jnp.concatenate([vt_ref[0, group(g), :], jnp.ones((ONES_ROWS, seq), BF16)], axis=0)
               for g in range(n_groups)]

    if moba:
        lane_k = lax.broadcasted_iota(jnp.int32, (seq, LANES), 1)
        key_blk = lax.broadcasted_iota(jnp.int32, (seq, LANES), 0) // MOBA_BLOCK
        first_k = lane_k < HEAD_DIM
        hot_hi = jnp.where(lane_k - HEAD_DIM == key_blk, 1.0, 0.0).astype(BF16)
        hot_lo = jnp.where(lane_k == key_blk, 1.0, 0.0).astype(BF16)
        ks = []
        for g in range(n_groups):
            k2 = k_ref[0, :, group(g)]
            ks.append((jnp.where(first_k, k2, hot_hi), jnp.where(first_k, hot_lo, k2)))
        km = kmean_ref[0]
        km_hi = km.astype(BF16)
        km_lo = (km - km_hi.astype(F32)).astype(BF16)
        nb = km.shape[0]
        blk = lax.broadcasted_iota(jnp.int32, (nb, tq), 0)
    else:
        ks = [(k_ref[0, :, group(g)],) * 2 for g in range(n_groups)]
        lam = (jnp.exp(jnp.sum(lq1_ref[...] * lk1_ref[...], axis=1, keepdims=True))
               - jnp.exp(jnp.sum(lq2_ref[...] * lk2_ref[...], axis=1, keepdims=True))
               + lambda_init)

    def block_mask(qt, j, g):
        rows = []
        for h in (1, 0):
            qh = jnp.where(own_rows[h], qt, jnp.zeros_like(qt))
            gate = (jnp.dot(km_hi[:, group(g)], qh, preferred_element_type=F32)
                    + jnp.dot(km_lo[:, group(g)], qh, preferred_element_type=F32))
            past = blk < j
            gate = jnp.where(past, gate, NEG_INF)
            beaten = jnp.zeros((nb, tq), jnp.int32)
            for m in range(j):
                gm = gate[m:m + 1, :]
                beats = (gm > gate) | ((gm == gate) & (blk > m))
                beaten = beaten + beats.astype(jnp.int32)
            dropped = past & (beaten >= MOBA_TOPK)
            rows.append(jnp.where(dropped, MASK_BIAS, 0.0).astype(F32))
            rows.append(jnp.zeros((HEAD_DIM - nb, tq), F32))
        return jnp.concatenate(rows, axis=0).astype(BF16)

    def query_side(j, g):
        qt = qt_ref[0, group(g), j * tq:(j + 1) * tq]
        other = block_mask(qt, j, g) if (moba and j > MOBA_TOPK) else jnp.zeros_like(qt)
        return [jnp.where(own_rows[h], qt, other) for h in range(2)]

    def scores(qa, j, g, h):
        return jnp.dot(ks[g][h][:(j + 1) * tq], qa, preferred_element_type=F32)

    def weights(s, j):
        length = (j + 1) * tq
        s_own = jnp.where(causal, s[length - tq:], NEG_INF)
        m = jnp.max(s_own, axis=0, keepdims=True)
        if j > 0:
            s_past = s[:length - tq]
            m = jnp.maximum(m, jnp.max(s_past, axis=0, keepdims=True))
        p = jnp.exp2(s_own - m)
        if j > 0:
            p = jnp.concatenate([jnp.exp2(s_past - m), p], axis=0)
        return p.astype(BF16)

    def finish(p_pair, j, g):
        length = (j + 1) * tq
        p_both = jnp.concatenate(p_pair, axis=1)
        acc = jnp.dot(vt_ones[g][:, :length], p_both, preferred_element_type=F32)
        o = acc[:LANES] / acc[LANES:LANES + 1]
        o0, o1 = o[:, :tq], o[:, tq:]
        if moba:
            out = jnp.where(first_rows, o0, o1).T
        else:
            out = _rmsnorm((o0 - lam * o1).T, gsub_ref[...]) * (1.0 - lambda_init)
        rows = slice(j * tq, (j + 1) * tq)
        o_ref[0, rows, group(g)] = (out * gz_ref[0, rows, group(g)]).astype(BF16)

    lookahead = MOBA_QK_LOOKAHEAD if moba else DIFF_QK_LOOKAHEAD
    units = [(j, g, h) for j in reversed(range(seq // tq)) for g in range(n_groups) for h in range(2)]
    pending = []

    def issue(idx):
        jn, gn, hn = units[idx]
        if hn == 0:
            issue.qa = query_side(jn, gn)
        pending.append(scores(issue.qa[hn], jn, gn, hn))

    for idx in range(min(lookahead, len(units))):
        issue(idx)
    p_pair = []
    for idx, (j, g, h) in enumerate(units):
        if idx + lookahead < len(units):
            issue(idx + lookahead)
        p_pair.append(weights(pending.pop(0), j))
        if h == 1:
            finish(p_pair, j, g)
            p_pair = []


def _attention(qt, k, vt, gz, extra, *, moba, lambda_init, col, groups):
    batch, seq, _ = k.shape
    per_step = MOBA_GROUPS_PER_STEP if moba else DIFF_GROUPS_PER_STEP
    width = per_step * LANES
    t_spec = pl.BlockSpec((1, width, seq), lambda b, g: (b, col // per_step + g, 0))
    row_spec = pl.BlockSpec((1, seq, width), lambda b, g: (b, 0, col // per_step + g))
    in_specs = [t_spec, row_spec, t_spec, row_spec]
    if moba:
        (kmean,) = extra
        in_specs.append(pl.BlockSpec((1, kmean.shape[1], width), lambda b, g: (b, 0, g)))
    else:
        for a in extra:
            in_specs.append(pl.BlockSpec(a.shape, lambda b, g: (0, 0)))
    return pl.pallas_call(
        functools.partial(_attn_kernel, moba=moba, lambda_init=lambda_init),
        grid=(batch, groups // per_step),
        in_specs=in_specs,
        out_specs=pl.BlockSpec((1, seq, width), lambda b, g: (b, 0, g)),
        out_shape=jax.ShapeDtypeStruct((batch, seq, groups * LANES), BF16),
        compiler_params=pltpu.CompilerParams(
            dimension_semantics=("arbitrary", "arbitrary"),
            vmem_limit_bytes=VMEM_LIMIT_BYTES),
        name="moba_attention" if moba else "diff_attention",
    )(qt, k, vt, gz, *extra)


def _outproj_kernel(x_ref, ym_ref, yd_ref, p_ref, wout_ref, wple_ref, wgate_ref,
                    gple_ref, gfin_ref, o_ref, *, final):
    mw = ym_ref.shape[1]
    tm = x_ref.shape[0]
    subs = [slice(r * tm // OUT_SUBTILES, (r + 1) * tm // OUT_SUBTILES) for r in range(OUT_SUBTILES)]
    hs = [x_ref[r, :]
          + jnp.dot(ym_ref[r, :], wout_ref[:mw, :], preferred_element_type=F32)
          + jnp.dot(yd_ref[r, :], wout_ref[mw:, :], preferred_element_type=F32) for r in subs]
    pes = [jnp.dot(p_ref[r, :].astype(BF16), wple_ref[...], preferred_element_type=F32) for r in subs]
    gates = [jax.nn.sigmoid(jnp.dot(_rmsnorm(h, gple_ref[...]).astype(BF16), wgate_ref[...],
                                    preferred_element_type=F32)) for h in hs]
    for r, h, gate, pe in zip(subs, hs, gates, pes):
        h = h + gate * pe
        o_ref[r, :] = _rmsnorm(h, gfin_ref[...]) if final else h


def _out_projection(x2d, y_moba, y_diff, p2d, w_out, w_ple, w_gate, g_ple, g_final, final):
    rows, d_model = x2d.shape
    tm = OUT_ROW_TILE
    row_spec = lambda a: pl.BlockSpec((tm, a.shape[1]), lambda i: (i, 0))
    full_spec = lambda a: pl.BlockSpec(a.shape, lambda i: (0, 0))
    return pl.pallas_call(
        functools.partial(_outproj_kernel, final=final),
        grid=(rows // tm,),
        in_specs=[row_spec(x2d), row_spec(y_moba), row_spec(y_diff), row_spec(p2d),
                  full_spec(w_out), full_spec(w_ple), full_spec(w_gate),
                  full_spec(g_ple), full_spec(g_final)],
        out_specs=pl.BlockSpec((tm, d_model), lambda i: (i, 0)),
        out_shape=jax.ShapeDtypeStruct((rows, d_model), F32),
        compiler_params=pltpu.CompilerParams(
            dimension_semantics=("arbitrary",), vmem_limit_bytes=VMEM_LIMIT_BYTES),
        name="out_projection",
    )(x2d, y_moba, y_diff, p2d, w_out, w_ple, w_gate, g_ple, g_final)


def kernel(x, p, w_in, w_out, g_mix, diff_lq1, diff_lk1, diff_lq2, diff_lk2, g_subln,
           w_ple, w_ple_gate, g_ple, g_final):
    batch, seq, d_model = x.shape
    depth = w_in.shape[0]
    rows = batch * seq
    assert w_in.shape[2] == QKV_W + MOBA_W + DIFF_W
    assert seq % ROW_TILE == 0 and ROW_TILE % MOBA_BLOCK == 0 and Q_TILE == MOBA_BLOCK
    rope_tab = _rope_tables(seq)
    row = lambda a: a.reshape(1, -1)
    h = x.reshape(rows, d_model)
    for i in range(depth):
        lambda_init = 0.8 - 0.6 * float(np.exp(-0.3 * i))
        qt, k, vt, gz, kmean = _in_projection(h, row(g_mix[i]), w_in[i].astype(BF16), rope_tab,
                                              batch, seq)
        k = k.reshape(batch, seq, 2 * SLAB)
        gz = gz.reshape(batch, seq, MOBA_W + DIFF_W)
        kmean = kmean.reshape(batch, seq // MOBA_BLOCK, SLAB)
        moba_groups = MOBA_W // LANES
        y_moba = _attention(qt, k, vt, gz, (kmean,), moba=True, lambda_init=lambda_init,
                            col=0, groups=moba_groups)
        y_diff = _attention(qt, k, vt, gz, (row(diff_lq1[i]), row(diff_lk1[i]), row(diff_lq2[i]),
                                            row(diff_lk2[i]), row(g_subln[i])),
                            moba=False, lambda_init=lambda_init,
                            col=moba_groups, groups=DIFF_W // LANES)
        h = _out_projection(h, y_moba.reshape(rows, MOBA_W), y_diff.reshape(rows, DIFF_W),
                            p[i].reshape(rows, -1), w_out[i].astype(BF16),
                            w_ple[i].astype(BF16), w_ple_gate[i].astype(BF16),
                            row(g_ple[i]), row(g_final), final=(i == depth - 1))
    return h.reshape(batch, seq, d_model)
```

```python
import functools

import numpy as np
import jax
import jax.numpy as jnp
from jax import lax
from jax.experimental import pallas as pl
from jax.experimental.pallas import tpu as pltpu

HEAD_DIM = 64
MOBA_HEADS = 8
MOBA_BLOCK = 256
MOBA_TOPK = 3
DIFF_HEADS = 4
ROPE_THETA = 500000.0
ROPE_DIM = HEAD_DIM // 4
EPS = 1e-6

LANES = 128
MOBA_W = MOBA_HEADS * HEAD_DIM
DIFF_W = DIFF_HEADS * 2 * HEAD_DIM
SLAB = 512
QKV_W = 6 * SLAB
ROW_TILE = 1024
OUT_ROW_TILE = 1024
OUT_SUBTILES = 4
Q_TILE = MOBA_BLOCK
MOBA_GROUPS_PER_STEP = 4
DIFF_GROUPS_PER_STEP = 4
MOBA_QK_LOOKAHEAD = 3
DIFF_QK_LOOKAHEAD = 5
VMEM_LIMIT_BYTES = 48 * 1024 * 1024

F32 = jnp.float32
BF16 = jnp.bfloat16
NEG_INF = float("-inf")
MASK_BIAS = -(2.0 ** 100)
ONES_ROWS = 16


def _rope_tables(seq):
    inv = ROPE_THETA ** (-jnp.arange(0, ROPE_DIM, 2, dtype=F32) / ROPE_DIM)
    ang = jnp.arange(seq, dtype=F32)[:, None] * inv[None, :]
    cos, sin = jnp.cos(ang), jnp.sin(ang)
    half = ROPE_DIM // 2
    ones = jnp.ones((seq, HEAD_DIM - ROPE_DIM), F32)
    zeros = lambda n: jnp.zeros((seq, n), F32)
    c = jnp.concatenate([cos, cos, ones], axis=-1)
    s1 = jnp.concatenate([-sin, zeros(HEAD_DIM - half)], axis=-1)
    s2 = jnp.concatenate([zeros(half), sin, zeros(HEAD_DIM - ROPE_DIM)], axis=-1)
    k_tabs = [jnp.concatenate([t, t], axis=-1) for t in (c, s1, s2)]
    scale = HEAD_DIM ** -0.5 * float(np.log2(np.e))
    q_tabs = [t * scale for t in k_tabs]
    return jnp.concatenate(q_tabs + k_tabs, axis=-1)


def _rmsnorm(x, g):
    return x * lax.rsqrt(jnp.mean(x * x, axis=-1, keepdims=True) + EPS) * g


def _inproj_kernel(x_ref, g_ref, w_ref, rope_ref, qt_ref, k_ref, vt_ref, gz_ref, kmean_ref):
    tm = x_ref.shape[0]

    def rope(slab, rows, tab0):
        outs = []
        for c in range(SLAB // LANES):
            t = slab[:, c * LANES:(c + 1) * LANES]
            cos = rope_ref[rows, (tab0 + 0) * LANES:(tab0 + 1) * LANES]
            s1 = rope_ref[rows, (tab0 + 1) * LANES:(tab0 + 2) * LANES]
            s2 = rope_ref[rows, (tab0 + 2) * LANES:(tab0 + 3) * LANES]
            half = ROPE_DIM // 2
            outs.append(t * cos + pltpu.roll(t, LANES - half, 1) * s1 + pltpu.roll(t, half, 1) * s2)
        return jnp.concatenate(outs, axis=-1)

    for blk in range(tm // MOBA_BLOCK):
        rows = slice(blk * MOBA_BLOCK, (blk + 1) * MOBA_BLOCK)
        u = _rmsnorm(x_ref[rows, :], g_ref[...]).astype(BF16)
        z = jnp.dot(u, w_ref[:, QKV_W:], preferred_element_type=F32)
        gz_ref[rows, :] = z * jax.nn.sigmoid(z)
        for sec in range(6):
            slab = jnp.dot(u, w_ref[:, sec * SLAB:(sec + 1) * SLAB], preferred_element_type=F32)
            head_cols = slice((sec // 3) * SLAB, (sec // 3 + 1) * SLAB)
            if sec % 3 == 0:
                qt_ref[0, head_cols, rows] = rope(slab, rows, 0).T.astype(BF16)
            elif sec % 3 == 1:
                slab = rope(slab, rows, 3)
                if sec == 1:
                    kmean_ref[blk] = jnp.mean(slab, axis=0, keepdims=True)
                k_ref[rows, head_cols] = slab.astype(BF16)
            else:
                vt_ref[0, head_cols, rows] = slab.T.astype(BF16)


def _in_projection(x2d, g_mix, w_in_bf16, rope_tab, batch, seq):
    rows, d_model = x2d.shape
    proj_w = w_in_bf16.shape[1]
    tm = ROW_TILE
    seq_tiles = seq // tm
    heads_w = 2 * SLAB
    t_spec = pl.BlockSpec((1, heads_w, tm), lambda i: (i // seq_tiles, 0, i % seq_tiles))
    return pl.pallas_call(
        _inproj_kernel,
        grid=(rows // tm,),
        in_specs=[
            pl.BlockSpec((tm, d_model), lambda i: (i, 0)),
            pl.BlockSpec((1, d_model), lambda i: (0, 0)),
            pl.BlockSpec((d_model, proj_w), lambda i: (0, 0), pipeline_mode=pl.Buffered(1)),
            pl.BlockSpec((tm, 6 * LANES), lambda i: (i % seq_tiles, 0)),
        ],
        out_specs=[
            t_spec,
            pl.BlockSpec((tm, heads_w), lambda i: (i, 0)),
            t_spec,
            pl.BlockSpec((tm, proj_w - QKV_W), lambda i: (i, 0)),
            pl.BlockSpec((tm // MOBA_BLOCK, 1, SLAB), lambda i: (i, 0, 0)),
        ],
        out_shape=[
            jax.ShapeDtypeStruct((batch, heads_w, seq), BF16),
            jax.ShapeDtypeStruct((rows, heads_w), BF16),
            jax.ShapeDtypeStruct((batch, heads_w, seq), BF16),
            jax.ShapeDtypeStruct((rows, proj_w - QKV_W), F32),
            jax.ShapeDtypeStruct((rows // MOBA_BLOCK, 1, SLAB), F32),
        ],
        compiler_params=pltpu.CompilerParams(
            dimension_semantics=("arbitrary",), vmem_limit_bytes=VMEM_LIMIT_BYTES),
        name="in_projection",
    )(x2d, g_mix, w_in_bf16, rope_tab)


def _attn_kernel(*refs, moba, lambda_init):
    if moba:
        qt_ref, k_ref, vt_ref, gz_ref, kmean_ref, o_ref = refs
    else:
        (qt_ref, k_ref, vt_ref, gz_ref, lq1_ref, lk1_ref, lq2_ref, lk2_ref, gsub_ref, o_ref,
         kbuf_ref) = refs
    seq = k_ref.shape[1]
    n_groups = k_ref.shape[2] // LANES
    tq = Q_TILE
    group = lambda g: slice(g * LANES, (g + 1) * LANES)

    first_rows = lax.broadcasted_iota(jnp.int32, (LANES, tq), 0) < HEAD_DIM
    own_rows = (first_rows, jnp.logical_not(first_rows))
    causal = (lax.broadcasted_iota(jnp.int32, (tq, tq), 0)
              <= lax.broadcasted_iota(jnp.int32, (tq, tq), 1))
    vt_ones = [jnp.concatenate([vt_ref[0, group(g), :], jnp.ones((ONES_ROWS, seq), BF16)], axis=0)
               for g in range(n_groups)]

    if moba:
        lane_k = lax.broadcasted_iota(jnp.int32, (seq, LANES), 1)
        key_blk = lax.broadcasted_iota(jnp.int32, (seq, LANES), 0) // MOBA_BLOCK
        first_k = lane_k < HEAD_DIM
        hot_hi = jnp.where(lane_k - HEAD_DIM == key_blk, 1.0, 0.0).astype(BF16)
        hot_lo = jnp.where(lane_k == key_blk, 1.0, 0.0).astype(BF16)
        ks = []
        for g in range(n_groups):
            k2 = k_ref[0, :, group(g)]
            ks.append((jnp.where(first_k, k2, hot_hi), jnp.where(first_k, hot_lo, k2)))
        km = kmean_ref[0]
        km_hi = km.astype(BF16)
        km_lo = (km - km_hi.astype(F32)).astype(BF16)
        nb = km.shape[0]
        blk = lax.broadcasted_iota(jnp.int32, (nb, tq), 0)
    else:
        for g in range(n_groups):
            kbuf_ref[g] = k_ref[0, :, group(g)]
        lam = (jnp.exp(jnp.sum(lq1_ref[...] * lk1_ref[...], axis=1, keepdims=True))
               - jnp.exp(jnp.sum(lq2_ref[...] * lk2_ref[...], axis=1, keepdims=True))
               + lambda_init)

    def block_mask(qt, j, g):
        rows = []
        for h in (1, 0):
            qh = jnp.where(own_rows[h], qt, jnp.zeros_like(qt))
            gate = (jnp.dot(km_hi[:, group(g)], qh, preferred_element_type=F32)
                    + jnp.dot(km_lo[:, group(g)], qh, preferred_element_type=F32))
            past = blk < j
            gate = jnp.where(past, gate, NEG_INF)
            beaten = jnp.zeros((nb, tq), jnp.int32)
            for m in range(j):
                gm = gate[m:m + 1, :]
                beats = (gm > gate) | ((gm == gate) & (blk > m))
                beaten = beaten + beats.astype(jnp.int32)
            dropped = past & (beaten >= MOBA_TOPK)
            rows.append(jnp.where(dropped, MASK_BIAS, 0.0).astype(F32))
            rows.append(jnp.zeros((HEAD_DIM - nb, tq), F32))
        return jnp.concatenate(rows, axis=0).astype(BF16)

    def query_side(j, g):
        qt = qt_ref[0, group(g), j * tq:(j + 1) * tq]
        other = block_mask(qt, j, g) if (moba and j > MOBA_TOPK) else jnp.zeros_like(qt)
        return [jnp.where(own_rows[h], qt, other) for h in range(2)]

    def scores(qa, j, g, h):
        length = (j + 1) * tq
        keys = ks[g][h][:length] if moba else kbuf_ref[g, :length, :]
        return jnp.dot(keys, qa, preferred_element_type=F32)

    def weights(s, j):
        length = (j + 1) * tq
        s_own = jnp.where(causal, s[length - tq:], NEG_INF)
        m = jnp.max(s_own, axis=0, keepdims=True)
        if j > 0:
            s_past = s[:length - tq]
            m = jnp.maximum(m, jnp.max(s_past, axis=0, keepdims=True))
        p = jnp.exp2(s_own - m)
        if j > 0:
            p = jnp.concatenate([jnp.exp2(s_past - m), p], axis=0)
        return p.astype(BF16)

    def finish(p_pair, j, g):
        length = (j + 1) * tq
        p_both = jnp.concatenate(p_pair, axis=1)
        acc = jnp.dot(vt_ones[g][:, :length], p_both, preferred_element_type=F32)
        o = acc[:LANES] / acc[LANES:LANES + 1]
        o0, o1 = o[:, :tq], o[:, tq:]
        if moba:
            out = jnp.where(first_rows, o0, o1).T
        else:
            out = _rmsnorm((o0 - lam * o1).T, gsub_ref[...]) * (1.0 - lambda_init)
        rows = slice(j * tq, (j + 1) * tq)
        o_ref[0, rows, group(g)] = (out * gz_ref[0, rows, group(g)]).astype(BF16)

    lookahead = MOBA_QK_LOOKAHEAD if moba else DIFF_QK_LOOKAHEAD
    units = [(j, g, h) for j in reversed(range(seq // tq)) for g in range(n_groups) for h in range(2)]
    pending = []

    def issue(idx):
        jn, gn, hn = units[idx]
        if hn == 0:
            issue.qa = query_side(jn, gn)
        pending.append(scores(issue.qa[hn], jn, gn, hn))

    for idx in range(min(lookahead, len(units))):
        issue(idx)
    p_pair = []
    for idx, (j, g, h) in enumerate(units):
        if idx + lookahead < len(units):
            issue(idx + lookahead)
        p_pair.append(weights(pending.pop(0), j))
        if h == 1:
            finish(p_pair, j, g)
            p_pair = []


def _attention(qt, k, vt, gz, extra, *, moba, lambda_init, col, groups):
    batch, seq, _ = k.shape
    per_step = MOBA_GROUPS_PER_STEP if moba else DIFF_GROUPS_PER_STEP
    width = per_step * LANES
    t_spec = pl.BlockSpec((1, width, seq), lambda b, g: (b, col // per_step + g, 0))
    row_spec = pl.BlockSpec((1, seq, width), lambda b, g: (b, 0, col // per_step + g))
    in_specs = [t_spec, row_spec, t_spec, row_spec]
    if moba:
        (kmean,) = extra
        in_specs.append(pl.BlockSpec((1, kmean.shape[1], width), lambda b, g: (b, 0, g)))
        scratch = []
    else:
        for a in extra:
            in_specs.append(pl.BlockSpec(a.shape, lambda b, g: (0, 0)))
        scratch = [pltpu.VMEM((per_step, seq, LANES), BF16)]
    return pl.pallas_call(
        functools.partial(_attn_kernel, moba=moba, lambda_init=lambda_init),
        grid=(batch, groups // per_step),
        in_specs=in_specs,
        out_specs=pl.BlockSpec((1, seq, width), lambda b, g: (b, 0, g)),
        out_shape=jax.ShapeDtypeStruct((batch, seq, groups * LANES), BF16),
        scratch_shapes=scratch,
        compiler_params=pltpu.CompilerParams(
            dimension_semantics=("arbitrary", "arbitrary"),
            vmem_limit_bytes=VMEM_LIMIT_BYTES),
        name="moba_attention" if moba else "diff_attention",
    )(qt, k, vt, gz, *extra)


def _outproj_kernel(x_ref, ym_ref, yd_ref, p_ref, wout_ref, wple_ref, wgate_ref,
                    gple_ref, gfin_ref, o_ref, *, final):
    mw = ym_ref.shape[1]
    tm = x_ref.shape[0]
    subs = [slice(r * tm // OUT_SUBTILES, (r + 1) * tm // OUT_SUBTILES) for r in range(OUT_SUBTILES)]
    hs = [x_ref[r, :]
          + jnp.dot(ym_ref[r, :], wout_ref[:mw, :], preferred_element_type=F32)
          + jnp.dot(yd_ref[r, :], wout_ref[mw:, :], preferred_element_type=F32) for r in subs]
    pes = [jnp.dot(p_ref[r, :].astype(BF16), wple_ref[...], preferred_element_type=F32) for r in subs]
    gates = [jax.nn.sigmoid(jnp.dot(_rmsnorm(h, gple_ref[...]).astype(BF16), wgate_ref[...],
                                    preferred_element_type=F32)) for h in hs]
    for r, h, gate, pe in zip(subs, hs, gates, pes):
        h = h + gate * pe
        o_ref[r, :] = _rmsnorm(h, gfin_ref[...]) if final else h


def _out_projection(x2d, y_moba, y_diff, p2d, w_out, w_ple, w_gate, g_ple, g_final, final):
    rows, d_model = x2d.shape
    tm = OUT_ROW_TILE
    row_spec = lambda a: pl.BlockSpec((tm, a.shape[1]), lambda i: (i, 0))
    full_spec = lambda a: pl.BlockSpec(a.shape, lambda i: (0, 0))
    return pl.pallas_call(
        functools.partial(_outproj_kernel, final=final),
        grid=(rows // tm,),
        in_specs=[row_spec(x2d), row_spec(y_moba), row_spec(y_diff), row_spec(p2d),
                  full_spec(w_out), full_spec(w_ple), full_spec(w_gate),
                  full_spec(g_ple), full_spec(g_final)],
        out_specs=pl.BlockSpec((tm, d_model), lambda i: (i, 0)),
        out_shape=jax.ShapeDtypeStruct((rows, d_model), F32),
        compiler_params=pltpu.CompilerParams(
            dimension_semantics=("arbitrary",), vmem_limit_bytes=VMEM_LIMIT_BYTES),
        name="out_projection",
    )(x2d, y_moba, y_diff, p2d, w_out, w_ple, w_gate, g_ple, g_final)


def kernel(x, p, w_in, w_out, g_mix, diff_lq1, diff_lk1, diff_lq2, diff_lk2, g_subln,
           w_ple, w_ple_gate, g_ple, g_final):
    batch, seq, d_model = x.shape
    depth = w_in.shape[0]
    rows = batch * seq
    assert w_in.shape[2] == QKV_W + MOBA_W + DIFF_W
    assert seq % ROW_TILE == 0 and ROW_TILE % MOBA_BLOCK == 0 and Q_TILE == MOBA_BLOCK
    rope_tab = _rope_tables(seq)
    row = lambda a: a.reshape(1, -1)
    h = x.reshape(rows, d_model)
    for i in range(depth):
        lambda_init = 0.8 - 0.6 * float(np.exp(-0.3 * i))
        qt, k, vt, gz, kmean = _in_projection(h, row(g_mix[i]), w_in[i].astype(BF16), rope_tab,
                                              batch, seq)
        k = k.reshape(batch, seq, 2 * SLAB)
        gz = gz.reshape(batch, seq, MOBA_W + DIFF_W)
        kmean = kmean.reshape(batch, seq // MOBA_BLOCK, SLAB)
        moba_groups = MOBA_W // LANES
        y_moba = _attention(qt, k, vt, gz, (kmean,), moba=True, lambda_init=lambda_init,
                            col=0, groups=moba_groups)
        y_diff = _attention(qt, k, vt, gz, (row(diff_lq1[i]), row(diff_lk1[i]), row(diff_lq2[i]),
                                            row(diff_lk2[i]), row(g_subln[i])),
                            moba=False, lambda_init=lambda_init,
                            col=moba_groups, groups=DIFF_W // LANES)
        h = _out_projection(h, y_moba.reshape(rows, MOBA_W), y_diff.reshape(rows, DIFF_W),
                            p[i].reshape(rows, -1), w_out[i].astype(BF16),
                            w_ple[i].astype(BF16), w_ple_gate[i].astype(BF16),
                            row(g_ple[i]), row(g_final), final=(i == depth - 1))
    return h.reshape(batch, seq, d_model)
```

```python
import functools

import numpy as np
import jax
import jax.numpy as jnp
from jax import lax
from jax.experimental import pallas as pl
from jax.experimental.pallas import tpu as pltpu

HEAD_DIM = 64
MOBA_HEADS = 8
MOBA_BLOCK = 256
MOBA_TOPK = 3
DIFF_HEADS = 4
ROPE_THETA = 500000.0
ROPE_DIM = HEAD_DIM // 4
EPS = 1e-6

LANES = 128
MOBA_W = MOBA_HEADS * HEAD_DIM
DIFF_W = DIFF_HEADS * 2 * HEAD_DIM
SLAB = 512
QKV_W = 6 * SLAB
ROW_TILE = 1024
OUT_ROW_TILE = 1024
OUT_SUBTILES = 4
Q_TILE = MOBA_BLOCK
ATTN_GROUPS_PER_STEP = 4
MOBA_QK_LOOKAHEAD = 3
DIFF_QK_LOOKAHEAD = 5
VMEM_LIMIT_BYTES = 48 * 1024 * 1024

F32 = jnp.float32
BF16 = jnp.bfloat16
NEG_INF = float("-inf")
MASK_BIAS = -(2.0 ** 100)
ONES_ROWS = 16


def _rope_tables(seq):
    inv = ROPE_THETA ** (-jnp.arange(0, ROPE_DIM, 2, dtype=F32) / ROPE_DIM)
    ang = jnp.arange(seq, dtype=F32)[:, None] * inv[None, :]
    cos, sin = jnp.cos(ang), jnp.sin(ang)
    half = ROPE_DIM // 2
    ones = jnp.ones((seq, HEAD_DIM - ROPE_DIM), F32)
    zeros = lambda n: jnp.zeros((seq, n), F32)
    c = jnp.concatenate([cos, cos, ones], axis=-1)
    s1 = jnp.concatenate([-sin, zeros(HEAD_DIM - half)], axis=-1)
    s2 = jnp.concatenate([zeros(half), sin, zeros(HEAD_DIM - ROPE_DIM)], axis=-1)
    k_tabs = [jnp.concatenate([t, t], axis=-1) for t in (c, s1, s2)]
    scale = HEAD_DIM ** -0.5 * float(np.log2(np.e))
    q_tabs = [t * scale for t in k_tabs]
    return jnp.concatenate(q_tabs + k_tabs, axis=-1)


def _rmsnorm(x, g):
    return x * lax.rsqrt(jnp.mean(x * x, axis=-1, keepdims=True) + EPS) * g


def _inproj_kernel(x_ref, g_ref, w_ref, rope_ref, qt_ref, k_ref, vt_ref, gz_ref, kmean_ref):
    tm = x_ref.shape[0]

    def rope(slab, rows, tab0):
        outs = []
        for c in range(SLAB // LANES):
            t = slab[:, c * LANES:(c + 1) * LANES]
            cos = rope_ref[rows, (tab0 + 0) * LANES:(tab0 + 1) * LANES]
            s1 = rope_ref[rows, (tab0 + 1) * LANES:(tab0 + 2) * LANES]
            s2 = rope_ref[rows, (tab0 + 2) * LANES:(tab0 + 3) * LANES]
            half = ROPE_DIM // 2
            outs.append(t * cos + pltpu.roll(t, LANES - half, 1) * s1 + pltpu.roll(t, half, 1) * s2)
        return jnp.concatenate(outs, axis=-1)

    for blk in range(tm // MOBA_BLOCK):
        rows = slice(blk * MOBA_BLOCK, (blk + 1) * MOBA_BLOCK)
        u = _rmsnorm(x_ref[rows, :], g_ref[...]).astype(BF16)
        z = jnp.dot(u, w_ref[:, QKV_W:], preferred_element_type=F32)
        gz_ref[rows, :] = z * jax.nn.sigmoid(z)
        for sec in range(6):
            slab = jnp.dot(u, w_ref[:, sec * SLAB:(sec + 1) * SLAB], preferred_element_type=F32)
            head_cols = slice((sec // 3) * SLAB, (sec // 3 + 1) * SLAB)
            if sec % 3 == 0:
                qt_ref[0, head_cols, rows] = rope(slab, rows, 0).T.astype(BF16)
            elif sec % 3 == 1:
                slab = rope(slab, rows, 3)
                if sec == 1:
                    kmean_ref[blk] = jnp.mean(slab, axis=0, keepdims=True)
                k_ref[rows, head_cols] = slab.astype(BF16)
            else:
                vt_ref[0, head_cols, rows] = slab.T.astype(BF16)


def _in_projection(x2d, g_mix, w_in_bf16, rope_tab, batch, seq):
    rows, d_model = x2d.shape
    proj_w = w_in_bf16.shape[1]
    tm = ROW_TILE
    seq_tiles = seq // tm
    heads_w = 2 * SLAB
    t_spec = pl.BlockSpec((1, heads_w, tm), lambda i: (i // seq_tiles, 0, i % seq_tiles))
    return pl.pallas_call(
        _inproj_kernel,
        grid=(rows // tm,),
        in_specs=[
            pl.BlockSpec((tm, d_model), lambda i: (i, 0)),
            pl.BlockSpec((1, d_model), lambda i: (0, 0)),
            pl.BlockSpec((d_model, proj_w), lambda i: (0, 0), pipeline_mode=pl.Buffered(1)),
            pl.BlockSpec((tm, 6 * LANES), lambda i: (i % seq_tiles, 0)),
        ],
        out_specs=[
            t_spec,
            pl.BlockSpec((tm, heads_w), lambda i: (i, 0)),
            t_spec,
            pl.BlockSpec((tm, proj_w - QKV_W), lambda i: (i, 0)),
            pl.BlockSpec((tm // MOBA_BLOCK, 1, SLAB), lambda i: (i, 0, 0)),
        ],
        out_shape=[
            jax.ShapeDtypeStruct((batch, heads_w, seq), BF16),
            jax.ShapeDtypeStruct((rows, heads_w), BF16),
            jax.ShapeDtypeStruct((batch, heads_w, seq), BF16),
            jax.ShapeDtypeStruct((rows, proj_w - QKV_W), F32),
            jax.ShapeDtypeStruct((rows // MOBA_BLOCK, 1, SLAB), F32),
        ],
        compiler_params=pltpu.CompilerParams(
            dimension_semantics=("arbitrary",), vmem_limit_bytes=VMEM_LIMIT_BYTES),
        name="in_projection",
    )(x2d, g_mix, w_in_bf16, rope_tab)


def _attn_kernel(*refs, moba, lambda_init):
    if moba:
        qt_ref, k_ref, vt_ref, gz_ref, kmean_ref, o_ref = refs
    else:
        qt_ref, k_ref, vt_ref, gz_ref, lq1_ref, lk1_ref, lq2_ref, lk2_ref, gsub_ref, o_ref = refs
    seq = k_ref.shape[1]
    n_groups = k_ref.shape[2] // LANES
    tq = Q_TILE
    group = lambda g: slice(g * LANES, (g + 1) * LANES)

    first_rows = lax.broadcasted_iota(jnp.int32, (LANES, tq), 0) < HEAD_DIM
    own_rows = (first_rows, jnp.logical_not(first_rows))
    causal = (lax.broadcasted_iota(jnp.int32, (tq, tq), 0)
              <= lax.broadcasted_iota(jnp.int32, (tq, tq), 1))
    vt_ones = [jnp.concatenate([vt_ref[0, group(g), :], jnp.ones((ONES_ROWS, seq), BF16)], axis=0)
               for g in range(n_groups)]

    if moba:
        lane_k = lax.broadcasted_iota(jnp.int32, (seq, LANES), 1)
        key_blk = lax.broadcasted_iota(jnp.int32, (seq, LANES), 0) // MOBA_BLOCK
        first_k = lane_k < HEAD_DIM
        hot_hi = jnp.where(lane_k - HEAD_DIM == key_blk, 1.0, 0.0).astype(BF16)
        hot_lo = jnp.where(lane_k == key_blk, 1.0, 0.0).astype(BF16)
        ks = []
        for g in range(n_groups):
            k2 = k_ref[0, :, group(g)]
            ks.append((jnp.where(first_k, k2, hot_hi), jnp.where(first_k, hot_lo, k2)))
        km = kmean_ref[0]
        km_hi = km.astype(BF16)
        km_lo = (km - km_hi.astype(F32)).astype(BF16)
        nb = km.shape[0]
        blk = lax.broadcasted_iota(jnp.int32, (nb, tq), 0)
    else:
        ks = [(k_ref[0, :, group(g)],) * 2 for g in range(n_groups)]
        lam = (jnp.exp(jnp.sum(lq1_ref[...] * lk1_ref[...], axis=1, keepdims=True))
               - jnp.exp(jnp.sum(lq2_ref[...] * lk2_ref[...], axis=1, keepdims=True))
               + lambda_init)

    def block_mask(qt, j, g):
        rows = []
        for h in (1, 0):
            qh = jnp.where(own_rows[h], qt, jnp.zeros_like(qt))
            gate = (jnp.dot(km_hi[:, group(g)], qh, preferred_element_type=F32)
                    + jnp.dot(km_lo[:, group(g)], qh, preferred_element_type=F32))
            past = blk < j
            gate = jnp.where(past, gate, NEG_INF)
            beaten = jnp.zeros((nb, tq), jnp.int32)
            for m in range(j):
                gm = gate[m:m + 1, :]
                beats = (gm > gate) | ((gm == gate) & (blk > m))
                beaten = beaten + beats.astype(jnp.int32)
            dropped = past & (beaten >= MOBA_TOPK)
            rows.append(jnp.where(dropped, MASK_BIAS, 0.0).astype(F32))
            rows.append(jnp.zeros((HEAD_DIM - nb, tq), F32))
        return jnp.concatenate(rows, axis=0).astype(BF16)

    def query_side(j, g):
        qt = qt_ref[0, group(g), j * tq:(j + 1) * tq]
        other = block_mask(qt, j, g) if (moba and j > MOBA_TOPK) else jnp.zeros_like(qt)
        return [jnp.where(own_rows[h], qt, other) for h in range(2)]

    def scores(qa, j, g, h):
        return jnp.dot(ks[g][h][:(j + 1) * tq], qa, preferred_element_type=F32)

    def weights(s, j):
        length = (j + 1) * tq
        s_own = jnp.where(causal, s[length - tq:], NEG_INF)
        m = jnp.max(s_own, axis=0, keepdims=True)
        if j > 0:
            s_past = s[:length - tq]
            m = jnp.maximum(m, jnp.max(s_past, axis=0, keepdims=True))
        p = jnp.exp2(s_own - m)
        if j > 0:
            p = jnp.concatenate([jnp.exp2(s_past - m), p], axis=0)
        return p.astype(BF16)

    def finish(p_pair, j, g):
        length = (j + 1) * tq
        p_both = jnp.concatenate(p_pair, axis=1)
        acc = jnp.dot(vt_ones[g][:, :length], p_both, preferred_element_type=F32)
        o = acc[:LANES] / acc[LANES:LANES + 1]
        o0, o1 = o[:, :tq], o[:, tq:]
        if moba:
            out = jnp.where(first_rows, o0, o1).T
        else:
            out = _rmsnorm((o0 - lam * o1).T, gsub_ref[...]) * (1.0 - lambda_init)
        rows = slice(j * tq, (j + 1) * tq)
        o_ref[0, rows, group(g)] = (out * gz_ref[0, rows, group(g)]).astype(BF16)

    lookahead = MOBA_QK_LOOKAHEAD if moba else DIFF_QK_LOOKAHEAD
    units = [(j, g, h) for j in reversed(range(seq // tq)) for g in range(n_groups) for h in range(2)]
    pending = []

    def issue(idx):
        jn, gn, hn = units[idx]
        if hn == 0:
            issue.qa = query_side(jn, gn)
        pending.append(scores(issue.qa[hn], jn, gn, hn))

    for idx in range(min(lookahead, len(units))):
        issue(idx)
    p_pair = []
    for idx, (j, g, h) in enumerate(units):
        if idx + lookahead < len(units):
            issue(idx + lookahead)
        p_pair.append(weights(pending.pop(0), j))
        if h == 1:
            finish(p_pair, j, g)
            p_pair = []


def _attention(qt, k, vt, gz, extra, *, moba, lambda_init, col, groups):
    batch, seq, _ = k.shape
    per_step = ATTN_GROUPS_PER_STEP
    width = per_step * LANES
    t_spec = pl.BlockSpec((1, width, seq), lambda b, g: (b, col // per_step + g, 0))
    row_spec = pl.BlockSpec((1, seq, width), lambda b, g: (b, 0, col // per_step + g))
    in_specs = [t_spec, row_spec, t_spec, row_spec]
    if moba:
        (kmean,) = extra
        in_specs.append(pl.BlockSpec((1, kmean.shape[1], width), lambda b, g: (b, 0, g)))
    else:
        for a in extra:
            in_specs.append(pl.BlockSpec(a.shape, lambda b, g: (0, 0)))
    return pl.pallas_call(
        functools.partial(_attn_kernel, moba=moba, lambda_init=lambda_init),
        grid=(batch, groups // per_step),
        in_specs=in_specs,
        out_specs=pl.BlockSpec((1, seq, width), lambda b, g: (b, 0, g)),
        out_shape=jax.ShapeDtypeStruct((batch, seq, groups * LANES), BF16),
        compiler_params=pltpu.CompilerParams(
            dimension_semantics=("arbitrary", "arbitrary"),
            vmem_limit_bytes=VMEM_LIMIT_BYTES),
        name="moba_attention" if moba else "diff_attention",
    )(qt, k, vt, gz, *extra)


def _outproj_kernel(x_ref, ym_ref, yd_ref, p_ref, wout_ref, wple_ref, wgate_ref,
                    gple_ref, gfin_ref, o_ref, *, final):
    mw = ym_ref.shape[1]
    tm = x_ref.shape[0]
    subs = [slice(r * tm // OUT_SUBTILES, (r + 1) * tm // OUT_SUBTILES) for r in range(OUT_SUBTILES)]
    hs = [x_ref[r, :]
          + jnp.dot(ym_ref[r, :], wout_ref[:mw, :], preferred_element_type=F32)
          + jnp.dot(yd_ref[r, :], wout_ref[mw:, :], preferred_element_type=F32) for r in subs]
    pes = [jnp.dot(p_ref[r, :].astype(BF16), wple_ref[...], preferred_element_type=F32) for r in subs]
    gates = [jax.nn.sigmoid(jnp.dot(_rmsnorm(h, gple_ref[...]).astype(BF16), wgate_ref[...],
                                    preferred_element_type=F32)) for h in hs]
    for r, h, gate, pe in zip(subs, hs, gates, pes):
        h = h + gate * pe
        o_ref[r, :] = _rmsnorm(h, gfin_ref[...]) if final else h


def _out_projection(x2d, y_moba, y_diff, p2d, w_out, w_ple, w_gate, g_ple, g_final, final):
    rows, d_model = x2d.shape
    tm = OUT_ROW_TILE
    row_spec = lambda a: pl.BlockSpec((tm, a.shape[1]), lambda i: (i, 0))
    full_spec = lambda a: pl.BlockSpec(a.shape, lambda i: (0, 0))
    return pl.pallas_call(
        functools.partial(_outproj_kernel, final=final),
        grid=(rows // tm,),
        in_specs=[row_spec(x2d), row_spec(y_moba), row_spec(y_diff), row_spec(p2d),
                  full_spec(w_out), full_spec(w_ple), full_spec(w_gate),
                  full_spec(g_ple), full_spec(g_final)],
        out_specs=pl.BlockSpec((tm, d_model), lambda i: (i, 0)),
        out_shape=jax.ShapeDtypeStruct((rows, d_model), F32),
        compiler_params=pltpu.CompilerParams(
            dimension_semantics=("arbitrary",), vmem_limit_bytes=VMEM_LIMIT_BYTES),
        name="out_projection",
    )(x2d, y_moba, y_diff, p2d, w_out, w_ple, w_gate, g_ple, g_final)


def kernel(x, p, w_in, w_out, g_mix, diff_lq1, diff_lk1, diff_lq2, diff_lk2, g_subln,
           w_ple, w_ple_gate, g_ple, g_final):
    batch, seq, d_model = x.shape
    depth = w_in.shape[0]
    rows = batch * seq
    assert w_in.shape[2] == QKV_W + MOBA_W + DIFF_W
    assert seq % ROW_TILE == 0 and ROW_TILE % MOBA_BLOCK == 0 and Q_TILE == MOBA_BLOCK
    with jax.ensure_compile_time_eval():
        rope_tab = _rope_tables(seq)
    row = lambda a: a.reshape(1, -1)
    h = x.reshape(rows, d_model)
    for i in range(depth):
        lambda_init = 0.8 - 0.6 * float(np.exp(-0.3 * i))
        qt, k, vt, gz, kmean = _in_projection(h, row(g_mix[i]), w_in[i].astype(BF16), rope_tab,
                                              batch, seq)
        k = k.reshape(batch, seq, 2 * SLAB)
        gz = gz.reshape(batch, seq, MOBA_W + DIFF_W)
        kmean = kmean.reshape(batch, seq // MOBA_BLOCK, SLAB)
        moba_groups = MOBA_W // LANES
        y_moba = _attention(qt, k, vt, gz, (kmean,), moba=True, lambda_init=lambda_init,
                            col=0, groups=moba_groups)
        y_diff = _attention(qt, k, vt, gz, (row(diff_lq1[i]), row(diff_lk1[i]), row(diff_lq2[i]),
                                            row(diff_lk2[i]), row(g_subln[i])),
                            moba=False, lambda_init=lambda_init,
                            col=moba_groups, groups=DIFF_W // LANES)
        h = _out_projection(h, y_moba.reshape(rows, MOBA_W), y_diff.reshape(rows, DIFF_W),
                            p[i].reshape(rows, -1), w_out[i].astype(BF16),
                            w_ple[i].astype(BF16), w_ple_gate[i].astype(BF16),
                            row(g_ple[i]), row(g_final), final=(i == depth - 1))
    return h.reshape(batch, seq, d_model)
```

```python
import functools

import numpy as np
import jax
import jax.numpy as jnp
from jax import lax
from jax.experimental import pallas as pl
from jax.experimental.pallas import tpu as pltpu

HEAD_DIM = 64
MOBA_HEADS = 8
MOBA_BLOCK = 256
MOBA_TOPK = 3
DIFF_HEADS = 4
ROPE_THETA = 500000.0
ROPE_DIM = HEAD_DIM // 4
EPS = 1e-6

LANES = 128
MOBA_W = MOBA_HEADS * HEAD_DIM
DIFF_W = DIFF_HEADS * 2 * HEAD_DIM
SLAB = 512
QKV_W = 6 * SLAB
ROW_TILE = 1024
OUT_ROW_TILE = 1024
OUT_SUBTILES = 4
Q_TILE = MOBA_BLOCK
MOBA_GROUPS_PER_STEP = 4
DIFF_GROUPS_PER_STEP = 2
MOBA_QK_LOOKAHEAD = 3
DIFF_QK_LOOKAHEAD = 5
VMEM_LIMIT_BYTES = 48 * 1024 * 1024

F32 = jnp.float32
BF16 = jnp.bfloat16
NEG_INF = float("-inf")
MASK_BIAS = -(2.0 ** 100)
ONES_ROWS = 16


def _rope_tables(seq):
    inv = ROPE_THETA ** (-jnp.arange(0, ROPE_DIM, 2, dtype=F32) / ROPE_DIM)
    ang = jnp.arange(seq, dtype=F32)[:, None] * inv[None, :]
    cos, sin = jnp.cos(ang), jnp.sin(ang)
    half = ROPE_DIM // 2
    ones = jnp.ones((seq, HEAD_DIM - ROPE_DIM), F32)
    zeros = lambda n: jnp.zeros((seq, n), F32)
    c = jnp.concatenate([cos, cos, ones], axis=-1)
    s1 = jnp.concatenate([-sin, zeros(HEAD_DIM - half)], axis=-1)
    s2 = jnp.concatenate([zeros(half), sin, zeros(HEAD_DIM - ROPE_DIM)], axis=-1)
    k_tabs = [jnp.concatenate([t, t], axis=-1) for t in (c, s1, s2)]
    scale = HEAD_DIM ** -0.5 * float(np.log2(np.e))
    q_tabs = [t * scale for t in k_tabs]
    return jnp.concatenate(q_tabs + k_tabs, axis=-1)


def _rmsnorm(x, g):
    return x * lax.rsqrt(jnp.mean(x * x, axis=-1, keepdims=True) + EPS) * g


def _inproj_kernel(x_ref, g_ref, w_ref, rope_ref, qt_ref, k_ref, vt_ref, gz_ref, kmean_ref):
    tm = x_ref.shape[0]

    def rope(slab, rows, tab0):
        outs = []
        for c in range(SLAB // LANES):
            t = slab[:, c * LANES:(c + 1) * LANES]
            cos = rope_ref[rows, (tab0 + 0) * LANES:(tab0 + 1) * LANES]
            s1 = rope_ref[rows, (tab0 + 1) * LANES:(tab0 + 2) * LANES]
            s2 = rope_ref[rows, (tab0 + 2) * LANES:(tab0 + 3) * LANES]
            half = ROPE_DIM // 2
            outs.append(t * cos + pltpu.roll(t, LANES - half, 1) * s1 + pltpu.roll(t, half, 1) * s2)
        return jnp.concatenate(outs, axis=-1)

    for blk in range(tm // MOBA_BLOCK):
        rows = slice(blk * MOBA_BLOCK, (blk + 1) * MOBA_BLOCK)
        u = _rmsnorm(x_ref[rows, :], g_ref[...]).astype(BF16)
        z = jnp.dot(u, w_ref[:, QKV_W:], preferred_element_type=F32)
        gz_ref[rows, :] = z * jax.nn.sigmoid(z)
        for sec in range(6):
            slab = jnp.dot(u, w_ref[:, sec * SLAB:(sec + 1) * SLAB], preferred_element_type=F32)
            head_cols = slice((sec // 3) * SLAB, (sec // 3 + 1) * SLAB)
            if sec % 3 == 0:
                qt_ref[0, head_cols, rows] = rope(slab, rows, 0).T.astype(BF16)
            elif sec % 3 == 1:
                slab = rope(slab, rows, 3)
                if sec == 1:
                    kmean_ref[blk] = jnp.mean(slab, axis=0, keepdims=True)
                k_ref[rows, head_cols] = slab.astype(BF16)
            else:
                vt_ref[0, head_cols, rows] = slab.T.astype(BF16)


def _in_projection(x2d, g_mix, w_in_bf16, rope_tab, batch, seq):
    rows, d_model = x2d.shape
    proj_w = w_in_bf16.shape[1]
    tm = ROW_TILE
    seq_tiles = seq // tm
    heads_w = 2 * SLAB
    t_spec = pl.BlockSpec((1, heads_w, tm), lambda i: (i // seq_tiles, 0, i % seq_tiles))
    return pl.pallas_call(
        _inproj_kernel,
        grid=(rows // tm,),
        in_specs=[
            pl.BlockSpec((tm, d_model), lambda i: (i, 0)),
            pl.BlockSpec((1, d_model), lambda i: (0, 0)),
            pl.BlockSpec((d_model, proj_w), lambda i: (0, 0), pipeline_mode=pl.Buffered(1)),
            pl.BlockSpec((tm, 6 * LANES), lambda i: (i % seq_tiles, 0)),
        ],
        out_specs=[
            t_spec,
            pl.BlockSpec((tm, heads_w), lambda i: (i, 0)),
            t_spec,
            pl.BlockSpec((tm, proj_w - QKV_W), lambda i: (i, 0)),
            pl.BlockSpec((tm // MOBA_BLOCK, 1, SLAB), lambda i: (i, 0, 0)),
        ],
        out_shape=[
            jax.ShapeDtypeStruct((batch, heads_w, seq), BF16),
            jax.ShapeDtypeStruct((rows, heads_w), BF16),
            jax.ShapeDtypeStruct((batch, heads_w, seq), BF16),
            jax.ShapeDtypeStruct((rows, proj_w - QKV_W), F32),
            jax.ShapeDtypeStruct((rows // MOBA_BLOCK, 1, SLAB), F32),
        ],
        compiler_params=pltpu.CompilerParams(
            dimension_semantics=("arbitrary",), vmem_limit_bytes=VMEM_LIMIT_BYTES),
        name="in_projection",
    )(x2d, g_mix, w_in_bf16, rope_tab)


def _attn_kernel(*refs, moba, lambda_init):
    if moba:
        qt_ref, k_ref, vt_ref, gz_ref, kmean_ref, o_ref = refs
    else:
        qt_ref, k_ref, vt_ref, gz_ref, lq1_ref, lk1_ref, lq2_ref, lk2_ref, gsub_ref, o_ref = refs
    seq = k_ref.shape[1]
    n_groups = k_ref.shape[2] // LANES
    tq = Q_TILE
    group = lambda g: slice(g * LANES, (g + 1) * LANES)

    first_rows = lax.broadcasted_iota(jnp.int32, (LANES, tq), 0) < HEAD_DIM
    own_rows = (first_rows, jnp.logical_not(first_rows))
    causal = (lax.broadcasted_iota(jnp.int32, (tq, tq), 0)
              <= lax.broadcasted_iota(jnp.int32, (tq, tq), 1))
    vt_ones = [jnp.concatenate([vt_ref[0, group(g), :], jnp.ones((ONES_ROWS, seq), BF16)], axis=0)
               for g in range(n_groups)]

    if moba:
        lane_k = lax.broadcasted_iota(jnp.int32, (seq, LANES), 1)
        key_blk = lax.broadcasted_iota(jnp.int32, (seq, LANES), 0) // MOBA_BLOCK
        first_k = lane_k < HEAD_DIM
        hot_hi = jnp.where(lane_k - HEAD_DIM == key_blk, 1.0, 0.0).astype(BF16)
        hot_lo = jnp.where(lane_k == key_blk, 1.0, 0.0).astype(BF16)
        ks = []
        for g in range(n_groups):
            k2 = k_ref[0, :, group(g)]
            ks.append((jnp.where(first_k, k2, hot_hi), jnp.where(first_k, hot_lo, k2)))
        km = kmean_ref[0]
        km_hi = km.astype(BF16)
        km_lo = (km - km_hi.astype(F32)).astype(BF16)
        nb = km.shape[0]
        blk = lax.broadcasted_iota(jnp.int32, (nb, tq), 0)
    else:
        ks = [(k_ref[0, :, group(g)],) * 2 for g in range(n_groups)]
        lam = (jnp.exp(jnp.sum(lq1_ref[...] * lk1_ref[...], axis=1, keepdims=True))
               - jnp.exp(jnp.sum(lq2_ref[...] * lk2_ref[...], axis=1, keepdims=True))
               + lambda_init)

    def block_mask(qt, j, g):
        rows = []
        for h in (1, 0):
            qh = jnp.where(own_rows[h], qt, jnp.zeros_like(qt))
            gate = (jnp.dot(km_hi[:, group(g)], qh, preferred_element_type=F32)
                    + jnp.dot(km_lo[:, group(g)], qh, preferred_element_type=F32))
            past = blk < j
            gate = jnp.where(past, gate, NEG_INF)
            beaten = jnp.zeros((nb, tq), jnp.int32)
            for m in range(j):
                gm = gate[m:m + 1, :]
                beats = (gm > gate) | ((gm == gate) & (blk > m))
                beaten = beaten + beats.astype(jnp.int32)
            dropped = past & (beaten >= MOBA_TOPK)
            rows.append(jnp.where(dropped, MASK_BIAS, 0.0).astype(F32))
            rows.append(jnp.zeros((HEAD_DIM - nb, tq), F32))
        return jnp.concatenate(rows, axis=0).astype(BF16)

    def query_side(j, g):
        qt = qt_ref[0, group(g), j * tq:(j + 1) * tq]
        other = block_mask(qt, j, g) if (moba and j > MOBA_TOPK) else jnp.zeros_like(qt)
        return [jnp.where(own_rows[h], qt, other) for h in range(2)]

    def scores(qa, j, g, h):
        return jnp.dot(ks[g][h][:(j + 1) * tq], qa, preferred_element_type=F32)

    def weights(s, j):
        length = (j + 1) * tq
        s_own = jnp.where(causal, s[length - tq:], NEG_INF)
        m = jnp.max(s_own, axis=0, keepdims=True)
        if j > 0:
            s_past = s[:length - tq]
            m = jnp.maximum(m, jnp.max(s_past, axis=0, keepdims=True))
        p = jnp.exp2(s_own - m)
        if j > 0:
            p = jnp.concatenate([jnp.exp2(s_past - m), p], axis=0)
        return p.astype(BF16)

    def finish(p_pair, j, g):
        length = (j + 1) * tq
        p_both = jnp.concatenate(p_pair, axis=1)
        acc = jnp.dot(vt_ones[g][:, :length], p_both, preferred_element_type=F32)
        o = acc[:LANES] / acc[LANES:LANES + 1]
        o0, o1 = o[:, :tq], o[:, tq:]
        if moba:
            out = jnp.where(first_rows, o0, o1).T
        else:
            out = _rmsnorm((o0 - lam * o1).T, gsub_ref[...]) * (1.0 - lambda_init)
        rows = slice(j * tq, (j + 1) * tq)
        o_ref[0, rows, group(g)] = (out * gz_ref[0, rows, group(g)]).astype(BF16)

    lookahead = MOBA_QK_LOOKAHEAD if moba else DIFF_QK_LOOKAHEAD
    units = [(j, g, h) for j in reversed(range(seq // tq)) for g in range(n_groups) for h in range(2)]
    pending = []

    def issue(idx):
        jn, gn, hn = units[idx]
        if hn == 0:
            issue.qa = query_side(jn, gn)
        pending.append(scores(issue.qa[hn], jn, gn, hn))

    for idx in range(min(lookahead, len(units))):
        issue(idx)
    p_pair = []
    for idx, (j, g, h) in enumerate(units):
        if idx + lookahead < len(units):
            issue(idx + lookahead)
        p_pair.append(weights(pending.pop(0), j))
        if h == 1:
            finish(p_pair, j, g)
            p_pair = []


def _attention(qt, k, vt, gz, extra, *, moba, lambda_init, col, groups):
    batch, seq, _ = k.shape
    per_step = MOBA_GROUPS_PER_STEP if moba else DIFF_GROUPS_PER_STEP
    width = per_step * LANES
    t_spec = pl.BlockSpec((1, width, seq), lambda b, g: (b, col // per_step + g, 0))
    row_spec = pl.BlockSpec((1, seq, width), lambda b, g: (b, 0, col // per_step + g))
    in_specs = [t_spec, row_spec, t_spec, row_spec]
    if moba:
        (kmean,) = extra
        in_specs.append(pl.BlockSpec((1, kmean.shape[1], width), lambda b, g: (b, 0, g)))
    else:
        for a in extra:
            in_specs.append(pl.BlockSpec(a.shape, lambda b, g: (0, 0)))
    return pl.pallas_call(
        functools.partial(_attn_kernel, moba=moba, lambda_init=lambda_init),
        grid=(batch, groups // per_step),
        in_specs=in_specs,
        out_specs=pl.BlockSpec((1, seq, width), lambda b, g: (b, 0, g)),
        out_shape=jax.ShapeDtypeStruct((batch, seq, groups * LANES), BF16),
        compiler_params=pltpu.CompilerParams(
            dimension_semantics=("arbitrary", "arbitrary"),
            vmem_limit_bytes=VMEM_LIMIT_BYTES),
        name="moba_attention" if moba else "diff_attention",
    )(qt, k, vt, gz, *extra)


def _outproj_kernel(x_ref, ym_ref, yd_ref, p_ref, wout_ref, wple_ref, wgate_ref,
                    gple_ref, gfin_ref, o_ref, *, final):
    mw = ym_ref.shape[1]
    tm = x_ref.shape[0]
    subs = [slice(r * tm // OUT_SUBTILES, (r + 1) * tm // OUT_SUBTILES) for r in range(OUT_SUBTILES)]
    hs = [x_ref[r, :]
          + jnp.dot(ym_ref[r, :], wout_ref[:mw, :], preferred_element_type=F32)
          + jnp.dot(yd_ref[r, :], wout_ref[mw:, :], preferred_element_type=F32) for r in subs]
    pes = [jnp.dot(p_ref[r, :].astype(BF16), wple_ref[...], preferred_element_type=F32) for r in subs]
    gates = [jax.nn.sigmoid(jnp.dot(_rmsnorm(h, gple_ref[...]).astype(BF16), wgate_ref[...],
                                    preferred_element_type=F32)) for h in hs]
    for r, h, gate, pe in zip(subs, hs, gates, pes):
        h = h + gate * pe
        o_ref[r, :] = _rmsnorm(h, gfin_ref[...]) if final else h


def _out_projection(x2d, y_moba, y_diff, p2d, w_out, w_ple, w_gate, g_ple, g_final, final):
    rows, d_model = x2d.shape
    tm = OUT_ROW_TILE
    row_spec = lambda a: pl.BlockSpec((tm, a.shape[1]), lambda i: (i, 0))
    full_spec = lambda a: pl.BlockSpec(a.shape, lambda i: (0, 0))
    return pl.pallas_call(
        functools.partial(_outproj_kernel, final=final),
        grid=(rows // tm,),
        in_specs=[row_spec(x2d), row_spec(y_moba), row_spec(y_diff), row_spec(p2d),
                  full_spec(w_out), full_spec(w_ple), full_spec(w_gate),
                  full_spec(g_ple), full_spec(g_final)],
        out_specs=pl.BlockSpec((tm, d_model), lambda i: (i, 0)),
        out_shape=jax.ShapeDtypeStruct((rows, d_model), F32),
        compiler_params=pltpu.CompilerParams(
            dimension_semantics=("arbitrary",), vmem_limit_bytes=VMEM_LIMIT_BYTES),
        name="out_projection",
    )(x2d, y_moba, y_diff, p2d, w_out, w_ple, w_gate, g_ple, g_final)


def kernel(x, p, w_in, w_out, g_mix, diff_lq1, diff_lk1, diff_lq2, diff_lk2, g_subln,
           w_ple, w_ple_gate, g_ple, g_final):
    batch, seq, d_model = x.shape
    depth = w_in.shape[0]
    rows = batch * seq
    assert w_in.shape[2] == QKV_W + MOBA_W + DIFF_W
    assert seq % ROW_TILE == 0 and ROW_TILE % MOBA_BLOCK == 0 and Q_TILE == MOBA_BLOCK
    with jax.ensure_compile_time_eval():
        rope_tab = _rope_tables(seq)
    row = lambda a: a.reshape(1, -1)
    h = x.reshape(rows, d_model)
    for i in range(depth):
        lambda_init = 0.8 - 0.6 * float(np.exp(-0.3 * i))
        qt, k, vt, gz, kmean = _in_projection(h, row(g_mix[i]), w_in[i].astype(BF16), rope_tab,
                                              batch, seq)
        k = k.reshape(batch, seq, 2 * SLAB)
        gz = gz.reshape(batch, seq, MOBA_W + DIFF_W)
        kmean = kmean.reshape(batch, seq // MOBA_BLOCK, SLAB)
        moba_groups = MOBA_W // LANES
        y_moba = _attention(qt, k, vt, gz, (kmean,), moba=True, lambda_init=lambda_init,
                            col=0, groups=moba_groups)
        y_diff = _attention(qt, k, vt, gz, (row(diff_lq1[i]), row(diff_lk1[i]), row(diff_lq2[i]),
                                            row(diff_lk2[i]), row(g_subln[i])),
                            moba=False, lambda_init=lambda_init,
                            col=moba_groups, groups=DIFF_W // LANES)
        h = _out_projection(h, y_moba.reshape(rows, MOBA_W), y_diff.reshape(rows, DIFF_W),
                            p[i].reshape(rows, -1), w_out[i].astype(BF16),
                            w_ple[i].astype(BF16), w_ple_gate[i].astype(BF16),
                            row(g_ple[i]), row(g_final), final=(i == depth - 1))
    return h.reshape(batch, seq, d_model)
```
